```python
import math
import jax, jax.numpy as jnp
from jax import lax
import numpy as np

D_MODEL = 1024
BATCH = 4
SEQ = 4096
DEPTH = 4
DEC_BATCH = 32
DEC_SEQ = 8
PAST_LEN = 8192
PAGE_SIZE = 128

HEAD_DIM = 64
MIX_W = D_MODEL
D_A = MIX_W // 4
H_A = D_A // HEAD_DIM
D_B = MIX_W // 2
H_B = D_B // HEAD_DIM
SSM_GROUPS = 2
N_STATE = 128
CONV_W = 4
CONV_DIM = D_B + 2 * SSM_GROUPS * N_STATE
SSD_CHUNK = 128
D_C = MIX_W - D_A - D_B
SCONV_W = 3
D_FF = 4 * D_MODEL
DILATED_CONFIGS = ((128, 1), (512, 4), (2048, 16))
WIN_MAX = 2048
IN_COLS = 3 * D_A + D_B + CONV_DIM + H_B + 3 * D_C
EPS = 1e-6

kernel_name = "hymba_dilated_ssd_shortconv_decoder_step"


def rms_norm(x, g):
    x32 = x.astype(jnp.float32)
    y = x32 * lax.rsqrt(jnp.mean(x32 * x32, axis=-1, keepdims=True) + EPS)
    return (y * g.astype(jnp.float32)).astype(x.dtype)


def alibi_slopes():
    return 2.0 ** (-8.0 * jnp.arange(1, H_A + 1, dtype=jnp.float32) / H_A)


def causal_conv(u, prev, w):
    width = w.shape[0]
    L = u.shape[1]
    ext = jnp.concatenate([prev.astype(u.dtype), u], axis=1)
    out = ext[:, 0:L] * w[0]
    for k in range(1, width):
        out = out + ext[:, k:k + L] * w[k]
    return out, ext[:, L:]


def dilated_branch_prompt(q, k, v, win, dil, slopes):
    Bsz, T, H, E = q.shape
    L = T // dil
    span = win // dil
    blk = span
    Lp = -(-L // blk) * blk
    nb = Lp // blk

    def to_strided(t):
        t = t.reshape(Bsz, L, dil, H, E).transpose(0, 2, 1, 3, 4)
        t = jnp.pad(t, ((0, 0), (0, 0), (0, Lp - L), (0, 0), (0, 0)))
        return t.reshape(Bsz, dil, nb, blk, H, E)

    qs, ks, vs = to_strided(q), to_strided(k), to_strided(v)
    pad = ((0, 0), (0, 0), (1, 0), (0, 0), (0, 0), (0, 0))
    k_ext, v_ext = jnp.pad(ks, pad), jnp.pad(vs, pad)
    k_band = jnp.concatenate([k_ext[:, :, :-1], k_ext[:, :, 1:]], axis=3)
    v_band = jnp.concatenate([v_ext[:, :, :-1], v_ext[:, :, 1:]], axis=3)

    s = jnp.einsum('bdnqhe,bdnkhe->bdnhqk', qs, k_band) * (HEAD_DIM ** -0.5)
    qi = jnp.arange(blk)
    ki = jnp.arange(2 * blk)
    diff = blk + qi[:, None] - ki[None, :]
    key_valid = (jnp.arange(nb)[:, None, None] > 0) | (ki[None, None, :] >= blk)
    mask = (diff >= 0)[None] & (diff <= span)[None] & key_valid
    dist = (dil * diff).astype(jnp.float32)
    s = s - slopes[:, None, None] * dist
    s = jnp.where(mask[None, None, :, None], s, -jnp.inf)
    m = jnp.max(s, axis=-1, keepdims=True)
    p = jnp.exp(s - m)
    l = jnp.sum(p, axis=-1)
    o = jnp.einsum('bdnhqk,bdnkhe->bdnqhe', p, v_band)
    o = o / jnp.transpose(l, (0, 1, 2, 4, 3))[..., None]

    o = o.reshape(Bsz, dil, Lp, H, E)[:, :, :L].transpose(0, 2, 1, 3, 4).reshape(Bsz, T, H, E)

    def stat_back(t):
        t = jnp.transpose(t, (0, 1, 2, 4, 3)).reshape(Bsz, dil, Lp, H)[:, :, :L]
        return t.transpose(0, 2, 1, 3).reshape(Bsz, T, H)

    return o, stat_back(m[..., 0]), stat_back(l)


def dilated_branch_sample(q, k_all, v_all, win, dil, slopes, wb):
    S = q.shape[1]
    span = win // dil
    j = jnp.arange(span + 1)
    idx = wb + jnp.arange(S)[:, None] - dil * j[None, :]
    valid = idx >= 0
    idx_c = jnp.maximum(idx, 0)
    kg = k_all[:, idx_c]
    vg = v_all[:, idx_c]
    s = jnp.einsum('bshe,bsjhe->bhsj', q, kg) * (HEAD_DIM ** -0.5)
    s = s - slopes[:, None, None] * (dil * j).astype(jnp.float32)[None, None, :]
    s = jnp.where(valid[None, None], s, -jnp.inf)
    m = jnp.max(s, axis=-1, keepdims=True)
    p = jnp.exp(s - m)
    l = jnp.sum(p, axis=-1)
    o = jnp.einsum('bhsj,bsjhe->bshe', p, vg)
    o = o / jnp.transpose(l, (0, 2, 1))[..., None]
    return o, jnp.transpose(m[..., 0], (0, 2, 1)), jnp.transpose(l, (0, 2, 1))


def merge_branches(results):
    os_ = jnp.stack([r[0] for r in results])
    ms = jnp.stack([r[1] for r in results])
    ls = jnp.stack([r[2] for r in results])
    m_all = jnp.max(ms, axis=0, keepdims=True)
    wts = ls * jnp.exp(ms - m_all)
    return jnp.sum(wts[..., None] * os_, axis=0) / jnp.sum(wts, axis=0)[..., None]


def ssd_scan(x, dt, a, bm, cm, h0, chunk):
    Bsz, L, H, P = x.shape
    G, N = bm.shape[2], bm.shape[3]
    nc = L // chunk
    f32 = jnp.float32
    x = x.astype(f32).reshape(Bsz, nc, chunk, H, P)
    dt = dt.reshape(Bsz, nc, chunk, H)
    bh = jnp.repeat(bm.astype(f32), H // G, axis=2).reshape(Bsz, nc, chunk, H, N)
    ch = jnp.repeat(cm.astype(f32), H // G, axis=2).reshape(Bsz, nc, chunk, H, N)
    acs = jnp.cumsum(dt * a, axis=2)
    causal = jnp.tril(jnp.ones((chunk, chunk), dtype=bool))
    seg = acs[:, :, :, None, :] - acs[:, :, None, :, :]
    decay = jnp.exp(jnp.where(causal[None, None, :, :, None], seg, -jnp.inf))
    scores = jnp.einsum('bclhn,bcshn->bclsh', ch, bh) * decay
    y_diag = jnp.einsum('bclsh,bcshp->bclhp', scores * dt[:, :, None, :, :], x)
    to_end = jnp.exp(acs[:, :, -1:, :] - acs)
    states = jnp.einsum('bclhn,bclhp->bchpn', bh * (to_end * dt)[..., None], x)
    chunk_decay = jnp.exp(acs[:, :, -1, :])

    def step(h, inp):
        dec, st = inp
        return dec[:, :, None, None] * h + st, h

    h_final, h_start = lax.scan(step, h0.astype(f32),
                                (jnp.moveaxis(chunk_decay, 1, 0), jnp.moveaxis(states, 1, 0)))
    h_start = jnp.moveaxis(h_start, 0, 1)
    y_off = jnp.einsum('bclhn,bchpn->bclhp', ch * jnp.exp(acs)[..., None], h_start)
    return (y_diag + y_off).reshape(Bsz, L, H, P), h_final


def token_mixer(h, w_in, w_out, attn_norm, conv_w, conv_b, a_log, dt_bias, d_skip, ssm_norm,
                sconv_w, sconv_norm, prev, prompt):
    Bsz, L, _ = h.shape
    dtype = h.dtype
    proj = h @ w_in
    cuts = list(np.cumsum([D_A, D_A, D_A, D_B, CONV_DIM, H_B, D_C, D_C]))
    q, k, v, z, xbc, dt_raw, b_gate, c_gate, u_c = jnp.split(proj, [int(c) for c in cuts], axis=-1)
    q = q.reshape(Bsz, L, H_A, HEAD_DIM).astype(jnp.float32)
    k = k.reshape(Bsz, L, H_A, HEAD_DIM)
    v = v.reshape(Bsz, L, H_A, HEAD_DIM)
    slopes = alibi_slopes()

    if prompt:
        res = [dilated_branch_prompt(q, k.astype(jnp.float32), v.astype(jnp.float32), w, d, slopes)
               for (w, d) in DILATED_CONFIGS]
        wb = min(WIN_MAX, L)
        new_k, new_v = k[:, L - wb:], v[:, L - wb:]
        h0 = jnp.zeros((Bsz, H_B, HEAD_DIM, N_STATE), jnp.float32)
        conv_prev = jnp.zeros((Bsz, CONV_W - 1, CONV_DIM), dtype)
        sconv_prev = jnp.zeros((Bsz, SCONV_W - 1, D_C), dtype)
    else:
        buf_k, buf_v, h0, conv_prev, sconv_prev = prev
        wb = buf_k.shape[1]
        k_all = jnp.concatenate([buf_k.astype(dtype), k], axis=1)
        v_all = jnp.concatenate([buf_v.astype(dtype), v], axis=1)
        res = [dilated_branch_sample(q, k_all.astype(jnp.float32), v_all.astype(jnp.float32),
                                     w, d, slopes, wb) for (w, d) in DILATED_CONFIGS]
        new_k, new_v = k_all[:, L:], v_all[:, L:]
    o_a = rms_norm(merge_branches(res).astype(dtype).reshape(Bsz, L, D_A), attn_norm)

    xbc_c, new_conv = causal_conv(xbc, conv_prev, conv_w)
    xbc_c = jax.nn.silu(xbc_c + conv_b)
    x_s, bm, cm = jnp.split(xbc_c, [D_B, D_B + SSM_GROUPS * N_STATE], axis=-1)
    dt = jax.nn.softplus(dt_raw.astype(jnp.float32) + dt_bias.astype(jnp.float32))
    a = -jnp.exp(a_log.astype(jnp.float32))
    x4 = x_s.reshape(Bsz, L, H_B, HEAD_DIM)
    chunk = min(SSD_CHUNK, L) if prompt else L
    y, h_new = ssd_scan(x4, dt, a, bm.reshape(Bsz, L, SSM_GROUPS, N_STATE),
                        cm.reshape(Bsz, L, SSM_GROUPS, N_STATE), h0, chunk)
    y = y + d_skip.astype(jnp.float32)[:, None] * x4.astype(jnp.float32)
    y_b = rms_norm(y.astype(dtype).reshape(Bsz, L, D_B) * jax.nn.silu(z), ssm_norm)

    conv_out, new_sconv = causal_conv(c_gate * u_c, sconv_prev, sconv_w)
    y_c = rms_norm(b_gate * conv_out, sconv_norm)

    out = jnp.concatenate([o_a, y_b, y_c], axis=-1) @ w_out
    return out, (new_k, new_v, h_new, new_conv, new_sconv)


def run_trunk(x, states, weights, prompt):
    (norm_mix_pre, norm_mix_post, norm_mlp_pre, norm_mlp_post, w_in, w_out, attn_norm,
     ssm_conv_w, ssm_conv_b, ssm_a_log, ssm_dt_bias, ssm_d, ssm_norm, sconv_w, sconv_norm,
     w_mlp_up, w_mlp_down) = weights
    per_layer = []
    for l in range(DEPTH):
        prev = None if prompt else tuple(s[l] for s in states)
        h = rms_norm(x, norm_mix_pre[l])
        mix, new = token_mixer(h, w_in[l], w_out[l], attn_norm[l], ssm_conv_w[l], ssm_conv_b[l],
                               ssm_a_log[l], ssm_dt_bias[l], ssm_d[l], ssm_norm[l], sconv_w[l],
                               sconv_norm[l], prev, prompt)
        x = x + rms_norm(mix, norm_mix_post[l])
        h = rms_norm(x, norm_mlp_pre[l])
        f = jnp.square(jax.nn.relu(h @ w_mlp_up[l])) @ w_mlp_down[l]
        x = x + rms_norm(f, norm_mlp_post[l])
        per_layer.append(new)
    stacked = tuple(jnp.stack([ns[i] for ns in per_layer]) for i in range(5))
    return x, stacked


def setup_inputs(seed: int = 0) -> dict:
    key = jax.random.key(seed)
    ks = jax.random.split(key, 24)
    f32 = jnp.float32

    def nrm(k, shape, scale):
        return scale * jax.random.normal(k, shape, f32)

    def gain(k, shape):
        return 1.0 + 0.02 * jax.random.normal(k, shape, f32)

    win_buf = min(WIN_MAX, PAST_LEN)
    dt0 = jnp.exp(jax.random.uniform(ks[17], (DEPTH, H_B), f32, math.log(1e-3), math.log(1e-1)))
    return {
        "x_prompt": nrm(ks[0], (BATCH, SEQ, D_MODEL), 1.0),
        "x_sample": nrm(ks[1], (DEC_BATCH, DEC_SEQ, D_MODEL), 1.0),
        "cache_attn_k": nrm(ks[2], (DEPTH, DEC_BATCH, win_buf, H_A, HEAD_DIM), 1.0),
        "cache_attn_v": nrm(ks[3], (DEPTH, DEC_BATCH, win_buf, H_A, HEAD_DIM), 1.0),
        "state_ssm": nrm(ks[4], (DEPTH, DEC_BATCH, H_B, HEAD_DIM, N_STATE), 0.1),
        "state_ssm_conv": nrm(ks[5], (DEPTH, DEC_BATCH, CONV_W - 1, CONV_DIM), 1.0),
        "state_sconv": nrm(ks[6], (DEPTH, DEC_BATCH, SCONV_W - 1, D_C), 1.0),
        "norm_mix_pre": gain(ks[7], (DEPTH, D_MODEL)),
        "norm_mix_post": gain(ks[8], (DEPTH, D_MODEL)),
        "norm_mlp_pre": gain(ks[9], (DEPTH, D_MODEL)),
        "norm_mlp_post": gain(ks[10], (DEPTH, D_MODEL)),
        "w_in": nrm(ks[11], (DEPTH, D_MODEL, IN_COLS), D_MODEL ** -0.5),
        "w_out": nrm(ks[12], (DEPTH, MIX_W, D_MODEL), MIX_W ** -0.5),
        "attn_norm": gain(ks[13], (DEPTH, D_A)),
        "ssm_conv_w": nrm(ks[14], (DEPTH, CONV_W, CONV_DIM), CONV_W ** -0.5),
        "ssm_conv_b": nrm(ks[15], (DEPTH, CONV_DIM), 0.01),
        "ssm_a_log": jnp.log(jax.random.uniform(ks[16], (DEPTH, H_B), f32, 1.0, 16.0)),
        "ssm_dt_bias": dt0 + jnp.log(-jnp.expm1(-dt0)),
        "ssm_d": gain(ks[18], (DEPTH, H_B)),
        "ssm_norm": gain(ks[19], (DEPTH, D_B)),
        "sconv_w": nrm(ks[20], (DEPTH, SCONV_W, D_C), SCONV_W ** -0.5),
        "sconv_norm": gain(ks[21], (DEPTH, D_C)),
        "w_mlp_up": nrm(ks[22], (DEPTH, D_MODEL, D_FF), D_MODEL ** -0.5),
        "w_mlp_down": nrm(ks[23], (DEPTH, D_FF, D_MODEL), D_FF ** -0.5),
    }


def reference(x_prompt, x_sample, cache_attn_k, cache_attn_v, state_ssm, state_ssm_conv,
              state_sconv, norm_mix_pre, norm_mix_post, norm_mlp_pre, norm_mlp_post, w_in, w_out,
              attn_norm, ssm_conv_w, ssm_conv_b, ssm_a_log, ssm_dt_bias, ssm_d, ssm_norm,
              sconv_w, sconv_norm, w_mlp_up, w_mlp_down):
    weights = (norm_mix_pre, norm_mix_post, norm_mlp_pre, norm_mlp_post, w_in, w_out, attn_norm,
               ssm_conv_w, ssm_conv_b, ssm_a_log, ssm_dt_bias, ssm_d, ssm_norm, sconv_w,
               sconv_norm, w_mlp_up, w_mlp_down)
    y_prompt, (p_k, p_v, p_ssm, p_conv, p_sconv) = run_trunk(x_prompt, None, weights, True)
    states = (cache_attn_k, cache_attn_v, state_ssm, state_ssm_conv, state_sconv)
    y_sample, (s_k, s_v, s_ssm, s_conv, s_sconv) = run_trunk(x_sample, states, weights, False)
    return (y_prompt, y_sample, p_k, p_v, p_ssm, p_conv, p_sconv, s_k, s_v, s_ssm, s_conv, s_sconv)
```

```python
import functools
import math

import jax
import jax.numpy as jnp
from jax import lax
from jax.experimental import pallas as pl
from jax.experimental.pallas import tpu as pltpu

F32 = jnp.float32
BF16 = jnp.bfloat16

HEAD_DIM = 64
D_MODEL = 1024
D_A = 256
H_A = 4
D_B = 512
H_B = 8
SSM_GROUPS = 2
N_STATE = 128
CONV_W = 4
CONV_DIM = D_B + 2 * SSM_GROUPS * N_STATE
D_C = 256
SCONV_W = 3
D_FF = 4 * D_MODEL
DILATED_CONFIGS = ((128, 1), (512, 4), (2048, 16))
WIN_MAX = 2048
EPS = 1e-6
ALIBI_SLOPES = tuple(2.0 ** (-8.0 * (h + 1) / H_A) for h in range(H_A))
ATTN_BLOCK = 128
SSD_CHUNK = 128

COL_XBC = 0
COL_Z = 1024
COL_Q = 1536
COL_G = 2304
COL_DT = 3072
PROJ_W = 3200

VMEM_LIMIT = 56 * 1024 * 1024


def _rms(x, g):
    return x * lax.rsqrt(jnp.mean(x * x, axis=-1, keepdims=True) + EPS) * g


def _silu(x):
    return x * (1.0 / (1.0 + jnp.exp(-x)))


def _softplus(x):
    return jnp.maximum(x, 0.0) + jnp.log1p(jnp.exp(-jnp.abs(x)))


def _in_proj_kernel(x_ref, g_ref, w_ref, o_ref, *, n_split):
    h = _rms(x_ref[...], g_ref[...]).astype(BF16)
    tn = w_ref.shape[1] // n_split
    for j in range(n_split):
        o_ref[:, j * tn:(j + 1) * tn] = jnp.dot(
            h, w_ref[:, j * tn:(j + 1) * tn], preferred_element_type=F32)


def _in_proj(x2d, g, w, layer, tm):
    m = x2d.shape[0]
    return pl.pallas_call(
        functools.partial(_in_proj_kernel, n_split=5),
        grid=(m // tm,),
        in_specs=[
            pl.BlockSpec((tm, D_MODEL), lambda i: (i, 0)),
            pl.BlockSpec((None, 1, D_MODEL), lambda i: (layer, 0, 0)),
            pl.BlockSpec((None, D_MODEL, PROJ_W), lambda i: (layer, 0, 0)),
        ],
        out_specs=pl.BlockSpec((tm, PROJ_W), lambda i: (i, 0)),
        out_shape=jax.ShapeDtypeStruct((m, PROJ_W), F32),
        compiler_params=pltpu.CompilerParams(
            dimension_semantics=("parallel",), vmem_limit_bytes=VMEM_LIMIT),
        name="in_proj",
    )(x2d, g, w)


def _attn_prompt_kernel(q0_ref, q1_ref, k0_ref, k1_ref, v0_ref, v1_ref, g_ref, o_ref,
                        acc_ref, m_ref, l_ref, *, seq):
    blk = ATTN_BLOCK
    shape = (blk, 2 * blk)
    lane_head = lax.broadcasted_iota(jnp.int32, shape, 1) // HEAD_DIM
    qi = lax.broadcasted_iota(jnp.int32, shape, 0)
    ki = lax.broadcasted_iota(jnp.int32, shape, 1)
    diff = blk + qi - ki
    band = (diff >= 0) & (diff <= blk)
    diff_f = diff.astype(F32)
    own = ki >= blk

    def load2(r0, r1, rows):
        return jnp.concatenate([r0[rows, :], r1[rows, :]], axis=1)

    def load_s(ref, rows):
        return jnp.concatenate([ref.at[0][rows, :], ref.at[1][rows, :]], axis=1)

    def store_s(ref, rows, val):
        ref.at[0][rows, :] = val[:, 0:128]
        ref.at[1][rows, :] = val[:, 128:256]

    for bi, (win, dil) in enumerate(DILATED_CONFIGS):
        assert win // dil == blk
        nblk = seq // dil // blk

        def body(it, carry, bi=bi, dil=dil, nblk=nblk):
            r = it // nblk
            n = it - r * nblk
            base = r + n * (blk * dil)
            prev = jnp.maximum(base - blk * dil, 0)
            if dil == 1:
                rows = pl.ds(pl.multiple_of(base, blk), blk)
                prows = pl.ds(pl.multiple_of(prev, blk), blk)
            else:
                rows = pl.ds(base, blk, stride=dil)
                prows = pl.ds(prev, blk, stride=dil)
            q = load2(q0_ref, q1_ref, rows) * (HEAD_DIM ** -0.5)
            kk = jnp.concatenate([load2(k0_ref, k1_ref, prows), load2(k0_ref, k1_ref, rows)],
                                 axis=0).astype(BF16)
            vv = jnp.concatenate([load2(v0_ref, v1_ref, prows), load2(v0_ref, v1_ref, rows)],
                                 axis=0).astype(BF16)
            mask = band & (own | (n > 0))
            acc_b = jnp.zeros(shape, F32)
            m_b = jnp.zeros(shape, F32)
            l_b = jnp.zeros(shape, F32)
            for h in range(H_A):
                hm = lane_head == h
                qh = jnp.where(hm, q, 0.0).astype(BF16)
                s = lax.dot_general(qh, kk, (((1,), (1,)), ((), ())),
                                    preferred_element_type=F32)
                s = s - (ALIBI_SLOPES[h] * dil) * diff_f
                s = jnp.where(mask, s, -jnp.inf)
                mh = jnp.max(s, axis=-1, keepdims=True)
                p = jnp.exp(s - mh)
                lh = jnp.sum(p, axis=-1, keepdims=True)
                oh = jnp.dot(p.astype(BF16), vv, preferred_element_type=F32)
                acc_b = jnp.where(hm, oh, acc_b)
                m_b = jnp.where(hm, mh, m_b)
                l_b = jnp.where(hm, lh, l_b)
            if bi == 0:
                store_s(acc_ref, rows, acc_b)
                store_s(m_ref, rows, m_b)
                store_s(l_ref, rows, l_b)
            else:
                m_old = load_s(m_ref, rows)
                m_new = jnp.maximum(m_old, m_b)
                a_old = jnp.exp(m_old - m_new)
                a_b = jnp.exp(m_b - m_new)
                store_s(acc_ref, rows, load_s(acc_ref, rows) * a_old + acc_b * a_b)
                store_s(l_ref, rows, load_s(l_ref, rows) * a_old + l_b * a_b)
                store_s(m_ref, rows, m_new)
            return carry

        lax.fori_loop(0, dil * nblk, body, 0)

    fin = 512

    def finish(i, carry):
        rows = pl.ds(pl.multiple_of(i * fin, fin), fin)
        o = load_s(acc_ref, rows) / load_s(l_ref, rows)
        o_ref[rows, :] = _rms(o, g_ref[...]).astype(o_ref.dtype)
        return carry

    lax.fori_loop(0, seq // fin, finish, 0)


def _attn_prompt(proj, g, layer, batch, seq):
    qb = COL_Q // 128
    halves = [pl.BlockSpec((seq, 128), functools.partial(lambda b, j: (b, j), j=qb + j))
              for j in range(6)]
    return pl.pallas_call(
        functools.partial(_attn_prompt_kernel, seq=seq),
        grid=(batch,),
        in_specs=halves + [pl.BlockSpec((None, 1, D_A), lambda b: (layer, 0, 0))],
        out_specs=pl.BlockSpec((seq, D_A), lambda b: (b, 0)),
        out_shape=jax.ShapeDtypeStruct((batch * seq, D_A), BF16),
        scratch_shapes=[pltpu.VMEM((2, seq, 128), F32)] * 3,
        compiler_params=pltpu.CompilerParams(
            dimension_semantics=("parallel",), vmem_limit_bytes=VMEM_LIMIT),
        name="attn_prompt",
    )(proj, proj, proj, proj, proj, proj, g)


def _attn_sample_kernel(q_ref, k_ref, v_ref, ck_ref, cv_ref, g_ref,
                        o_ref, nk_ref, nv_ref, kpad_ref, vpad_ref, *, wb, s_len):
    pad = 128
    k_new = k_ref[...]
    v_new = v_ref[...]
    nk_ref[0:wb - s_len, :] = ck_ref[s_len:wb, :]
    nk_ref[wb - s_len:wb, :] = k_new
    nv_ref[0:wb - s_len, :] = cv_ref[s_len:wb, :]
    nv_ref[wb - s_len:wb, :] = v_new

    kpad_ref[...] = jnp.zeros(kpad_ref.shape, F32)
    vpad_ref[...] = jnp.zeros(vpad_ref.shape, F32)
    kpad_ref[0:s_len, :] = k_new
    vpad_ref[0:s_len, :] = v_new

    rows = H_A * s_len
    ncol = wb + pad
    row_i = lax.broadcasted_iota(jnp.int32, (rows, ncol), 0)
    col_i = lax.broadcasted_iota(jnp.int32, (rows, ncol), 1)
    s_i = row_i % s_len
    d = wb + s_i - col_i
    mult = jnp.zeros((rows, ncol), F32)
    for win, dil in DILATED_CONFIGS:
        hit = (d >= 0) & (d <= win) & ((d % dil) == 0)
        mult = mult + jnp.where(hit, 1.0, 0.0)
    slope = jnp.zeros((rows, ncol), F32)
    for h in range(H_A):
        slope = jnp.where(row_i // s_len == h, ALIBI_SLOPES[h], slope)
    bias = slope * d.astype(F32)

    q = q_ref[...] * (HEAD_DIM ** -0.5)
    lane_head = lax.broadcasted_iota(jnp.int32, (s_len, D_A), 1) // HEAD_DIM
    qh = jnp.concatenate([jnp.where(lane_head == h, q, 0.0) for h in range(H_A)],
                         axis=0).astype(BF16)
    nt = (((1,), (1,)), ((), ()))
    s1 = lax.dot_general(qh, ck_ref[...].astype(BF16), nt, preferred_element_type=F32)
    s2 = lax.dot_general(qh, kpad_ref[...].astype(BF16), nt, preferred_element_type=F32)
    s = jnp.concatenate([s1, s2], axis=1) - bias
    s = jnp.where(mult > 0.0, s, -jnp.inf)
    m = jnp.max(s, axis=-1, keepdims=True)
    p = jnp.exp(s - m) * mult
    l = jnp.sum(p, axis=-1, keepdims=True)
    pb = p.astype(BF16)
    o = jnp.dot(pb[:, 0:wb], cv_ref[...].astype(BF16), preferred_element_type=F32)
    o = o + jnp.dot(pb[:, wb:ncol], vpad_ref[...].astype(BF16), preferred_element_type=F32)
    o = o / l
    out = jnp.zeros((s_len, D_A), F32)
    for h in range(H_A):
        out = jnp.where(lane_head == h, o[h * s_len:(h + 1) * s_len, :], out)
    o_ref[...] = _rms(out, g_ref[...]).astype(o_ref.dtype)


def _attn_sample(proj, cache_k, cache_v, g, layer, batch, s_len, wb):
    qb = COL_Q // D_A
    cache_spec = pl.BlockSpec((None, None, wb, D_A), lambda b: (layer, b, 0, 0))
    new_spec = pl.BlockSpec((None, wb, D_A), lambda b: (b, 0, 0))
    return pl.pallas_call(
        functools.partial(_attn_sample_kernel, wb=wb, s_len=s_len),
        grid=(batch,),
        in_specs=[
            pl.BlockSpec((s_len, D_A), lambda b: (b, qb)),
            pl.BlockSpec((s_len, D_A), lambda b: (b, qb + 1)),
            pl.BlockSpec((s_len, D_A), lambda b: (b, qb + 2)),
            cache_spec, cache_spec,
            pl.BlockSpec((None, 1, D_A), lambda b: (layer, 0, 0)),
        ],
        out_specs=[pl.BlockSpec((s_len, D_A), lambda b: (b, 0)), new_spec, new_spec],
        out_shape=[jax.ShapeDtypeStruct((batch * s_len, D_A), BF16),
                   jax.ShapeDtypeStruct((batch, wb, D_A), F32),
                   jax.ShapeDtypeStruct((batch, wb, D_A), F32)],
        scratch_shapes=[pltpu.VMEM((128, D_A), F32)] * 2,
        compiler_params=pltpu.CompilerParams(
            dimension_semantics=("parallel",), vmem_limit_bytes=VMEM_LIMIT),
        name="attn_sample",
    )(proj, proj, proj, cache_k, cache_v, g)


def _ssd_kernel(xbc_ref, z_ref, gate_ref, dtr_ref, h0_ref, cprev_ref, sprev_ref,
                cw_ref, cb_ref, alog_ref, dtb_ref, dsk_ref, sn_ref, scw_ref, scn_ref,
                y_ref, hout_ref, cnew_ref, snew_ref,
                ext_ref, ext2_ref, h_ref, *, chunk, n_valid):
    c = pl.program_id(1)
    nc = pl.num_programs(1)
    C = chunk
    heads_per_group = H_B // SSM_GROUPS
    gw = heads_per_group * HEAD_DIM

    @pl.when(c == 0)
    def _init():
        h_ref[...] = h0_ref[...]
        ext_ref[0:8, :] = cprev_ref[...]
        ext2_ref[0:8, :] = sprev_ref[...]

    xbc = xbc_ref[...]
    ext_ref[8:8 + C, :] = xbc
    conv = xbc * cw_ref[CONV_W - 1:CONV_W, :]
    for k in range(CONV_W - 1):
        off = 8 - (CONV_W - 1) + k
        conv = conv + ext_ref[off:off + C, :] * cw_ref[k:k + 1, :]
    xc = _silu(conv + cb_ref[...])
    x_s = xc[:, 0:D_B]
    bm = xc[:, D_B:D_B + SSM_GROUPS * N_STATE]
    cm = xc[:, D_B + SSM_GROUPS * N_STATE:CONV_DIM]

    row = lax.broadcasted_iota(jnp.int32, (C, C), 0)
    col = lax.broadcasted_iota(jnp.int32, (C, C), 1)
    causal = row >= col
    dt = _softplus(dtr_ref[...] + dtb_ref[...])
    if n_valid < C:
        rvalid = lax.broadcasted_iota(jnp.int32, dt.shape, 0) < n_valid
        dt = jnp.where(rvalid, dt, 0.0)
    a = -jnp.exp(alog_ref[...])
    tri = jnp.where(causal, 1.0, 0.0)
    acs = jnp.dot(tri, dt * a, preferred_element_type=F32,
                  precision=lax.Precision.HIGHEST)
    acs_t = acs.T
    dt_t = dt.T
    e_acs = jnp.exp(acs)
    last = acs[C - 1:C, :]
    w_end = jnp.exp(last - acs) * dt
    cdec = jnp.exp(last)

    lane_head = lax.broadcasted_iota(jnp.int32, (C, D_B), 1) // HEAD_DIM
    e_acs_x = jnp.zeros((C, D_B), F32)
    w_end_x = jnp.zeros((C, D_B), F32)
    for h in range(H_B):
        sel = lane_head == h
        e_acs_x = jnp.where(sel, e_acs[:, h:h + 1], e_acs_x)
        w_end_x = jnp.where(sel, w_end[:, h:h + 1], w_end_x)

    nt = (((1,), (1,)), ((), ()))
    tn = (((0,), (0,)), ((), ()))
    x_bf = x_s.astype(BF16)
    xw = (x_s * w_end_x)
    y_parts = []
    lane_lo = lax.broadcasted_iota(jnp.int32, (C, 2 * HEAD_DIM), 1) < HEAD_DIM
    for g in range(SSM_GROUPS):
        bm_g = bm[:, g * N_STATE:(g + 1) * N_STATE].astype(BF16)
        cm_g = cm[:, g * N_STATE:(g + 1) * N_STATE].astype(BF16)
        scores = lax.dot_general(cm_g, bm_g, nt, preferred_element_type=F32)
        h_g = h_ref[g * heads_per_group:(g + 1) * heads_per_group].reshape(gw, N_STATE)
        y_off = lax.dot_general(cm_g, h_g.astype(BF16), nt, preferred_element_type=F32)
        y_off = y_off * e_acs_x[:, g * gw:(g + 1) * gw]
        diag = []
        for pair in range(heads_per_group // 2):
            outs = []
            for hh in range(2):
                h = g * heads_per_group + 2 * pair + hh
                seg = acs[:, h:h + 1] - acs_t[h:h + 1, :]
                decay = jnp.exp(jnp.where(causal, seg, -jnp.inf))
                mh = (scores * decay * dt_t[h:h + 1, :]).astype(BF16)
                lo = (g * heads_per_group + 2 * pair) * HEAD_DIM
                outs.append(jnp.dot(mh, x_bf[:, lo:lo + 2 * HEAD_DIM],
                                    preferred_element_type=F32))
            diag.append(jnp.where(lane_lo, outs[0], outs[1]))
        y_parts.append(jnp.concatenate(diag, axis=1) + y_off)
        st = lax.dot_general(xw[:, g * gw:(g + 1) * gw].astype(BF16), bm_g, tn,
                             preferred_element_type=F32)
        for hh in range(heads_per_group):
            h = g * heads_per_group + hh
            h_ref[h] = h_ref[h] * cdec[0:1, h:h + 1] + st[hh * HEAD_DIM:(hh + 1) * HEAD_DIM, :]
    y = jnp.concatenate(y_parts, axis=1) + dsk_ref[...] * x_s
    y_b = _rms(y * _silu(z_ref[...]), sn_ref[...])

    gates = gate_ref[...]
    b_gate = gates[:, 0:D_C]
    prod = gates[:, D_C:2 * D_C] * gates[:, 2 * D_C:3 * D_C]
    ext2_ref[8:8 + C, :] = prod
    sconv = prod * scw_ref[SCONV_W - 1:SCONV_W, :]
    for k in range(SCONV_W - 1):
        off = 8 - (SCONV_W - 1) + k
        sconv = sconv + ext2_ref[off:off + C, :] * scw_ref[k:k + 1, :]
    y_c = _rms(b_gate * sconv, scn_ref[...])

    y_ref[:, 0:D_B] = y_b.astype(y_ref.dtype)
    y_ref[:, D_B:D_B + D_C] = y_c.astype(y_ref.dtype)

    @pl.when(c == nc - 1)
    def _final():
        hout_ref[...] = h_ref[...]
        cnew_ref[...] = ext_ref[n_valid:n_valid + 8, :]
        snew_ref[...] = ext2_ref[n_valid:n_valid + 8, :]

    ext_ref[0:8, :] = ext_ref[C:C + 8, :]
    ext2_ref[0:8, :] = ext2_ref[C:C + 8, :]


def _ssd(proj, h0, cprev8, sprev8, wts, layer, batch, nc, chunk, n_valid):
    (cw, cb, alog, dtb, dsk, sn, scw, scn) = wts

    def tok(width, blk_idx):
        return pl.BlockSpec((chunk, width), lambda b, c: (b * nc + c, blk_idx))

    def per_layer(shape):
        return pl.BlockSpec((None,) + shape, lambda b, c: (layer,) + (0,) * len(shape))

    state_spec = pl.BlockSpec((None, H_B, HEAD_DIM, N_STATE), lambda b, c: (b, 0, 0, 0))
    conv_spec = pl.BlockSpec((None, 8, CONV_DIM), lambda b, c: (b, 0, 0))
    sconv_spec = pl.BlockSpec((None, 8, D_C), lambda b, c: (b, 0, 0))
    return pl.pallas_call(
        functools.partial(_ssd_kernel, chunk=chunk, n_valid=n_valid),
        grid=(batch, nc),
        in_specs=[
            tok(CONV_DIM, COL_XBC // CONV_DIM),
            tok(D_B, COL_Z // D_B),
            tok(3 * D_C, COL_G // (3 * D_C)),
            tok(128, COL_DT // 128),
            state_spec, conv_spec, sconv_spec,
            per_layer((CONV_W, CONV_DIM)), per_layer((1, CONV_DIM)),
            per_layer((1, 128)), per_layer((1, 128)), per_layer((1, D_B)),
            per_layer((1, D_B)), per_layer((SCONV_W, D_C)), per_layer((1, D_C)),
        ],
        out_specs=[
            pl.BlockSpec((chunk, D_B + D_C), lambda b, c: (b * nc + c, 0)),
            state_spec, conv_spec, sconv_spec,
        ],
        out_shape=[
            jax.ShapeDtypeStruct((batch * nc * chunk, D_B + D_C), BF16),
            jax.ShapeDtypeStruct((batch, H_B, HEAD_DIM, N_STATE), F32),
            jax.ShapeDtypeStruct((batch, 8, CONV_DIM), F32),
            jax.ShapeDtypeStruct((batch, 8, D_C), F32),
        ],
        scratch_shapes=[
            pltpu.VMEM((chunk + 8, CONV_DIM), F32),
            pltpu.VMEM((chunk + 8, D_C), F32),
            pltpu.VMEM((H_B, HEAD_DIM, N_STATE), F32),
        ],
        compiler_params=pltpu.CompilerParams(
            dimension_semantics=("parallel", "arbitrary"), vmem_limit_bytes=VMEM_LIMIT),
        name="ssd_sconv",
    )(proj, proj, proj, proj, h0, cprev8, sprev8, cw, cb, alog, dtb, dsk, sn, scw, scn)


def _post_kernel(x_ref, oa_ref, ybc_ref, woa_ref, wobc_ref, gpost_ref, gpre_ref, gmlp_ref,
                 wup_ref, wdn_ref, o_ref, x1_ref, h_ref, acc_ref):
    j = pl.program_id(1)
    nj = pl.num_programs(1)

    @pl.when(j == 0)
    def _start():
        mix = jnp.dot(oa_ref[...], woa_ref[...], preferred_element_type=F32)
        mix = mix + jnp.dot(ybc_ref[...], wobc_ref[...], preferred_element_type=F32)
        x1 = x_ref[...] + _rms(mix, gpost_ref[...])
        x1_ref[...] = x1
        h_ref[...] = _rms(x1, gpre_ref[...]).astype(BF16)
        acc_ref[...] = jnp.zeros(acc_ref.shape, F32)

    u = jnp.dot(h_ref[...], wup_ref[...], preferred_element_type=F32)
    u = jnp.square(jnp.maximum(u, 0.0)).astype(BF16)
    acc_ref[...] += jnp.dot(u, wdn_ref[...], preferred_element_type=F32)

    @pl.when(j == nj - 1)
    def _finish():
        o_ref[...] = x1_ref[...] + _rms(acc_ref[...], gmlp_ref[...])


def _post(x2d, oa, ybc, w_out_a, w_out_bc, gpost, gpre, gmlp, w_up, w_dn, layer, tm, tf):
    m = x2d.shape[0]

    def gain():
        return pl.BlockSpec((None, 1, D_MODEL), lambda i, j: (layer, 0, 0))

    return pl.pallas_call(
        _post_kernel,
        grid=(m // tm, D_FF // tf),
        in_specs=[
            pl.BlockSpec((tm, D_MODEL), lambda i, j: (i, 0)),
            pl.BlockSpec((tm, D_A), lambda i, j: (i, 0)),
            pl.BlockSpec((tm, D_B + D_C), lambda i, j: (i, 0)),
            pl.BlockSpec((None, D_A, D_MODEL), lambda i, j: (layer, 0, 0)),
            pl.BlockSpec((None, D_B + D_C, D_MODEL), lambda i, j: (layer, 0, 0)),
            gain(), gain(), gain(),
            pl.BlockSpec((None, D_MODEL, tf), lambda i, j: (layer, 0, j)),
            pl.BlockSpec((None, tf, D_MODEL), lambda i, j: (layer, j, 0)),
        ],
        out_specs=pl.BlockSpec((tm, D_MODEL), lambda i, j: (i, 0)),
        out_shape=jax.ShapeDtypeStruct((m, D_MODEL), F32),
        scratch_shapes=[
            pltpu.VMEM((tm, D_MODEL), F32),
            pltpu.VMEM((tm, D_MODEL), BF16),
            pltpu.VMEM((tm, D_MODEL), F32),
        ],
        compiler_params=pltpu.CompilerParams(
            dimension_semantics=("parallel", "arbitrary"), vmem_limit_bytes=VMEM_LIMIT),
        name="post",
    )(x2d, oa, ybc, w_out_a, w_out_bc, gpost, gpre, gmlp, w_up, w_dn)


def _prep_weights(norm_mix_pre, norm_mix_post, norm_mlp_pre, norm_mlp_post, w_in, w_out,
                  attn_norm, ssm_conv_w, ssm_conv_b, ssm_a_log, ssm_dt_bias, ssm_d, ssm_norm,
                  sconv_w, sconv_norm, w_mlp_up, w_mlp_down):
    depth = w_in.shape[0]
    o = 0
    cols = {}
    for name, width in (("q", D_A), ("k", D_A), ("v", D_A), ("z", D_B), ("xbc", CONV_DIM),
                        ("dt", H_B), ("b", D_C), ("c", D_C), ("u", D_C)):
        cols[name] = (o, o + width)
        o += width

    def cs(name):
        lo, hi = cols[name]
        return w_in[:, :, lo:hi]

    pad_dt = jnp.zeros((depth, D_MODEL, 128 - H_B), w_in.dtype)
    w_in_r = jnp.concatenate(
        [cs("xbc"), cs("z"), cs("q"), cs("k"), cs("v"), cs("b"), cs("c"), cs("u"),
         cs("dt"), pad_dt], axis=2).astype(BF16)

    def lane_pad(v):
        return jnp.pad(v.astype(F32), ((0, 0), (0, 128 - H_B)))[:, None, :]

    def row(v):
        return v.astype(F32)[:, None, :]

    return dict(
        g_mix_pre=row(norm_mix_pre), g_mix_post=row(norm_mix_post),
        g_mlp_pre=row(norm_mlp_pre), g_mlp_post=row(norm_mlp_post),
        w_in=w_in_r,
        w_out_a=w_out[:, 0:D_A, :].astype(BF16),
        w_out_bc=w_out[:, D_A:, :].astype(BF16),
        attn_norm=row(attn_norm),
        ssd=(ssm_conv_w.astype(F32), row(ssm_conv_b), lane_pad(ssm_a_log),
             lane_pad(ssm_dt_bias), row(jnp.repeat(ssm_d, HEAD_DIM, axis=1)),
             row(ssm_norm), sconv_w.astype(F32), row(sconv_norm)),
        w_up=w_mlp_up.astype(BF16), w_dn=w_mlp_down.astype(BF16),
    )


def _run_trunk(x, states, w, prompt):
    batch, seq, _ = x.shape
    depth = w["w_in"].shape[0]
    m = batch * seq
    x2d = x.reshape(m, D_MODEL)
    tm = 512 if m % 512 == 0 else m
    outs = [[] for _ in range(5)]
    if prompt:
        chunk, nc, n_valid = SSD_CHUNK, seq // SSD_CHUNK, SSD_CHUNK
        wb = min(WIN_MAX, seq)
    else:
        chunk, nc, n_valid = SSD_CHUNK, 1, seq
        cache_k, cache_v, st_ssm, st_conv, st_sconv = states
        wb = cache_k.shape[2]
        cache_k = cache_k.reshape(depth, batch, wb, D_A)
        cache_v = cache_v.reshape(depth, batch, wb, D_A)
    for l in range(depth):
        proj = _in_proj(x2d, w["g_mix_pre"], w["w_in"], l, tm)
        if prompt:
            oa = _attn_prompt(proj, w["attn_norm"], l, batch, seq)
            p3 = proj.reshape(batch, seq, PROJ_W)
            new_k = p3[:, seq - wb:, COL_Q + D_A:COL_Q + 2 * D_A]
            new_v = p3[:, seq - wb:, COL_Q + 2 * D_A:COL_Q + 3 * D_A]
            h0 = jnp.zeros((batch, H_B, HEAD_DIM, N_STATE), F32)
            cprev8 = jnp.zeros((batch, 8, CONV_DIM), F32)
            sprev8 = jnp.zeros((batch, 8, D_C), F32)
            proj_ssd = proj
        else:
            oa, new_k, new_v = _attn_sample(proj, cache_k, cache_v, w["attn_norm"], l,
                                            batch, seq, wb)
            h0 = st_ssm[l].astype(F32)
            cprev8 = jnp.pad(st_conv[l].astype(F32), ((0, 0), (8 - (CONV_W - 1), 0), (0, 0)))
            sprev8 = jnp.pad(st_sconv[l].astype(F32), ((0, 0), (8 - (SCONV_W - 1), 0), (0, 0)))
            proj_ssd = jnp.pad(proj.reshape(batch, seq, PROJ_W),
                               ((0, 0), (0, chunk - seq), (0, 0))).reshape(batch * chunk, PROJ_W)
        ybc, h_new, c8, s8 = _ssd(proj_ssd, h0, cprev8, sprev8, w["ssd"], l, batch, nc,
                                  chunk, n_valid)
        if not prompt:
            ybc = ybc.reshape(batch, chunk, D_B + D_C)[:, :seq].reshape(m, D_B + D_C)
        x2d = _post(x2d, oa, ybc, w["w_out_a"], w["w_out_bc"], w["g_mix_post"],
                    w["g_mlp_pre"], w["g_mlp_post"], w["w_up"], w["w_dn"], l, tm, 1024)
        outs[0].append(new_k.reshape(batch, wb, H_A, HEAD_DIM))
        outs[1].append(new_v.reshape(batch, wb, H_A, HEAD_DIM))
        outs[2].append(h_new)
        outs[3].append(c8[:, 8 - (CONV_W - 1):, :])
        outs[4].append(s8[:, 8 - (SCONV_W - 1):, :])
    stacked = tuple(jnp.stack(o) for o in outs)
    return x2d.reshape(batch, seq, D_MODEL), stacked


def kernel(x_prompt, x_sample, cache_attn_k, cache_attn_v, state_ssm, state_ssm_conv, state_sconv, norm_mix_pre, norm_mix_post, norm_mlp_pre, norm_mlp_post, w_in, w_out, attn_norm, ssm_conv_w, ssm_conv_b, ssm_a_log, ssm_dt_bias, ssm_d, ssm_norm, sconv_w, sconv_norm, w_mlp_up, w_mlp_down):
    w = _prep_weights(norm_mix_pre, norm_mix_post, norm_mlp_pre, norm_mlp_post, w_in, w_out,
                      attn_norm, ssm_conv_w, ssm_conv_b, ssm_a_log, ssm_dt_bias, ssm_d,
                      ssm_norm, sconv_w, sconv_norm, w_mlp_up, w_mlp_down)
    y_prompt, (p_k, p_v, p_ssm, p_conv, p_sconv) = _run_trunk(x_prompt, None, w, True)
    states = (cache_attn_k, cache_attn_v, state_ssm, state_ssm_conv, state_sconv)
    y_sample, (s_k, s_v, s_ssm, s_conv, s_sconv) = _run_trunk(x_sample, states, w, False)
    return (y_prompt, y_sample, p_k, p_v, p_ssm, p_conv, p_sconv,
            s_k, s_v, s_ssm, s_conv, s_sconv)
```

```python
import functools
import math

import jax
import jax.numpy as jnp
from jax import lax
from jax.experimental import pallas as pl
from jax.experimental.pallas import tpu as pltpu

F32 = jnp.float32
BF16 = jnp.bfloat16

HEAD_DIM = 64
D_MODEL = 1024
D_A = 256
H_A = 4
D_B = 512
H_B = 8
SSM_GROUPS = 2
N_STATE = 128
CONV_W = 4
CONV_DIM = D_B + 2 * SSM_GROUPS * N_STATE
D_C = 256
SCONV_W = 3
D_FF = 4 * D_MODEL
DILATED_CONFIGS = ((128, 1), (512, 4), (2048, 16))
WIN_MAX = 2048
EPS = 1e-6
LOG2E = 1.4426950408889634
ALIBI_SLOPES = tuple(2.0 ** (-8.0 * (h + 1) / H_A) for h in range(H_A))
ATTN_BLOCK = 128
ATTN_PERM = 16
SSD_CHUNK = 128

COL_XBC = 0
COL_Z = 1024
COL_Q = 1536
COL_G = 2304
COL_DT = 3072
PROJ_W = 3200

VMEM_LIMIT = 56 * 1024 * 1024


def _rms(x, g):
    return x * lax.rsqrt(jnp.mean(x * x, axis=-1, keepdims=True) + EPS) * g


def _silu(x):
    return x * (1.0 / (1.0 + jnp.exp(-x)))


def _softplus(x):
    return jnp.maximum(x, 0.0) + jnp.log1p(jnp.exp(-jnp.abs(x)))


def _in_proj_kernel(x_ref, g_ref, w_ref, o_ref, *rest, n_split, permute):
    h = _rms(x_ref[...], g_ref[...]).astype(BF16)
    tn = w_ref.shape[1] // n_split
    for j in range(n_split):
        o_ref[:, j * tn:(j + 1) * tn] = jnp.dot(
            h, w_ref[:, j * tn:(j + 1) * tn], preferred_element_type=F32)
    if permute:
        operm_ref, scr_ref = rest
        tm = x_ref.shape[0]
        for j in range(3 * D_A // 128):
            scr_ref[j] = o_ref[:, COL_Q + j * 128:COL_Q + (j + 1) * 128]
        for r in range(ATTN_PERM):
            for j in range(3 * D_A // 128):
                operm_ref[r, :, j * 128:(j + 1) * 128] = (
                    scr_ref.at[j][pl.ds(r, tm // ATTN_PERM, stride=ATTN_PERM), :])


def _in_proj(x2d, g, w, layer, tm, seq=None):
    m = x2d.shape[0]
    permute = seq is not None
    out_specs = [pl.BlockSpec((tm, PROJ_W), lambda i: (i, 0))]
    out_shape = [jax.ShapeDtypeStruct((m, PROJ_W), F32)]
    scratch = []
    if permute:
        tiles = seq // tm
        rows = tm // ATTN_PERM
        out_specs.append(pl.BlockSpec((None, ATTN_PERM, rows, 3 * D_A),
                                      lambda i: (i // tiles, 0, i % tiles, 0)))
        out_shape.append(jax.ShapeDtypeStruct((m // seq, ATTN_PERM, seq // ATTN_PERM, 3 * D_A), F32))
        scratch.append(pltpu.VMEM((3 * D_A // 128, tm, 128), F32))
    return pl.pallas_call(
        functools.partial(_in_proj_kernel, n_split=5, permute=permute),
        grid=(m // tm,),
        in_specs=[
            pl.BlockSpec((tm, D_MODEL), lambda i: (i, 0)),
            pl.BlockSpec((None, 1, D_MODEL), lambda i: (layer, 0, 0)),
            pl.BlockSpec((None, D_MODEL, PROJ_W), lambda i: (layer, 0, 0)),
        ],
        out_specs=out_specs,
        out_shape=out_shape,
        scratch_shapes=scratch,
        compiler_params=pltpu.CompilerParams(
            dimension_semantics=("parallel",), vmem_limit_bytes=VMEM_LIMIT),
        name="in_proj",
    )(x2d, g, w)


def _attn_prompt_kernel(q_ref, k_ref, v_ref, g_ref, o0_ref, o1_ref,
                        acc_ref, m_ref, l_ref, bias_ref, *, seq):
    blk = ATTN_BLOCK
    nbr = len(DILATED_CONFIGS)
    P = ATTN_PERM
    per_res = seq // P

    def local_to_strided(a, dil):
        nchunk = P // dil
        rows = blk // nchunk
        return nchunk * (a % rows) + a // rows

    shape4 = (H_A * blk, 2 * blk)
    row4 = lax.broadcasted_iota(jnp.int32, shape4, 0)
    ki = lax.broadcasted_iota(jnp.int32, shape4, 1)
    slope = jnp.zeros(shape4, F32)
    for h in range(H_A):
        slope = jnp.where(row4 // blk == h, ALIBI_SLOPES[h], slope)
    for bi, (win, dil) in enumerate(DILATED_CONFIGS):
        assert win // dil == blk and P % dil == 0
        jq = blk + local_to_strided(row4 % blk, dil)
        jk = local_to_strided(ki % blk, dil) + blk * (ki // blk)
        diff = jq - jk
        band = (diff >= 0) & (diff <= blk)
        dist = slope * (-float(dil) * LOG2E * diff.astype(F32))
        bias_ref[bi] = jnp.where(band, dist, -jnp.inf)
        bias_ref[nbr + bi] = jnp.where(band & (ki >= blk), dist, -jnp.inf)

    lane_head = lax.broadcasted_iota(jnp.int32, (blk, D_A), 1) // HEAD_DIM
    lane_lo = lax.broadcasted_iota(jnp.int32, (blk, 128), 1) < HEAD_DIM

    def per_head(col):
        c = [col[h * blk:(h + 1) * blk, :] for h in range(H_A)]
        return jnp.concatenate([jnp.where(lane_lo, c[0], c[1]), jnp.where(lane_lo, c[2], c[3])],
                               axis=1)

    for bi, (win, dil) in enumerate(DILATED_CONFIGS):
        nchunk = P // dil
        rows = blk // nchunk
        npair = per_res // rows // 2

        def body(it, carry, bi=bi, dil=dil, nchunk=nchunk, rows=rows, npair=npair):
            r = it // npair
            u = it - r * npair
            starts = [pl.multiple_of(jnp.maximum(2 * u - 1, 0) * rows, rows),
                      pl.multiple_of(2 * u * rows, rows),
                      pl.multiple_of((2 * u + 1) * rows, rows)]

            def load(ref, st):
                return jnp.concatenate(
                    [ref[dil * c + r, pl.ds(st, rows), :] for c in range(nchunk)], axis=0)

            def store(ref, st, val):
                for c in range(nchunk):
                    ref[dil * c + r, pl.ds(st, rows), :] = val[c * rows:(c + 1) * rows, :]

            kb = [load(k_ref, st).astype(BF16) for st in starts]
            vb = [load(v_ref, st).astype(BF16) for st in starts]
            results = []
            for j in range(2):
                q = (load(q_ref, starts[1 + j]) * (HEAD_DIM ** -0.5 * LOG2E)).astype(BF16)
                q4 = jnp.concatenate(
                    [jnp.where(lane_head == h, q, jnp.zeros_like(q)) for h in range(H_A)], axis=0)
                kk = jnp.concatenate([kb[j], kb[j + 1]], axis=0)
                vv = jnp.concatenate([vb[j], vb[j + 1]], axis=0)
                s = lax.dot_general(q4, kk, (((1,), (1,)), ((), ())), preferred_element_type=F32)
                if j == 0:
                    s = s + bias_ref[jnp.where(u == 0, nbr + bi, bi)]
                else:
                    s = s + bias_ref[bi]
                m4 = jnp.max(s, axis=-1, keepdims=True)
                p = jnp.exp2(s - m4)
                l4 = jnp.sum(p, axis=-1, keepdims=True)
                pb = p.astype(BF16)
                o = [jnp.dot(pb[h * blk:(h + 1) * blk, :],
                             vv[:, (h // 2) * 128:(h // 2 + 1) * 128],
                             preferred_element_type=F32) for h in range(H_A)]
                acc_b = jnp.concatenate([jnp.where(lane_lo, o[0], o[1]),
                                         jnp.where(lane_lo, o[2], o[3])], axis=1)
                m_b = per_head(m4)
                l_b = per_head(l4)
                if bi > 0:
                    st = starts[1 + j]
                    m_old = load(m_ref, st)
                    m_new = jnp.maximum(m_old, m_b)
                    a_old = jnp.exp2(m_old - m_new)
                    a_b = jnp.exp2(m_b - m_new)
                    acc_b = load(acc_ref, st) * a_old + acc_b * a_b
                    l_b = load(l_ref, st) * a_old + l_b * a_b
                    m_b = m_new
                results.append((acc_b, m_b, l_b))
            for j in range(2):
                store(acc_ref, starts[1 + j], results[j][0])
                store(m_ref, starts[1 + j], results[j][1])
                store(l_ref, starts[1 + j], results[j][2])
            return carry

        lax.fori_loop(0, dil * npair, body, 0)

    def finish(r, carry):
        o = _rms(acc_ref[r] / l_ref[r], g_ref[...])
        o0_ref[pl.ds(r, per_res, stride=P), :] = o[:, 0:128]
        o1_ref[pl.ds(r, per_res, stride=P), :] = o[:, 128:256]
        return carry

    lax.fori_loop(0, P, finish, 0)


def _attn_prompt(qkv_perm, g, layer, batch, seq):
    per_res = seq // ATTN_PERM
    qkv_specs = [pl.BlockSpec((None, ATTN_PERM, per_res, D_A),
                              functools.partial(lambda b, j: (b, 0, 0, j), j=j)) for j in range(3)]
    half_spec = pl.BlockSpec((seq, 128), lambda b: (b, 0))
    return pl.pallas_call(
        functools.partial(_attn_prompt_kernel, seq=seq),
        grid=(batch,),
        in_specs=qkv_specs + [pl.BlockSpec((None, 1, D_A), lambda b: (layer, 0, 0))],
        out_specs=[half_spec, half_spec],
        out_shape=[jax.ShapeDtypeStruct((batch * seq, 128), F32)] * 2,
        scratch_shapes=[pltpu.VMEM((ATTN_PERM, per_res, D_A), F32)] * 3
        + [pltpu.VMEM((2 * len(DILATED_CONFIGS), H_A * ATTN_BLOCK, 2 * ATTN_BLOCK), F32)],
        compiler_params=pltpu.CompilerParams(
            dimension_semantics=("parallel",), vmem_limit_bytes=VMEM_LIMIT),
        name="attn_prompt",
    )(qkv_perm, qkv_perm, qkv_perm, g)


def _attn_sample_kernel(q_ref, k_ref, v_ref, ck_ref, cv_ref, g_ref, nk_in_ref, nv_in_ref,
                        o0_ref, o1_ref, nk_ref, nv_ref, kpad_ref, vpad_ref, *, wb, s_len):
    pad = 128
    k_new = k_ref[...]
    v_new = v_ref[...]
    nk_ref[0:wb - s_len, :] = ck_ref[s_len:wb, :]
    nk_ref[wb - s_len:wb, :] = k_new
    nv_ref[0:wb - s_len, :] = cv_ref[s_len:wb, :]
    nv_ref[wb - s_len:wb, :] = v_new

    kpad_ref[...] = jnp.zeros(kpad_ref.shape, F32)
    vpad_ref[...] = jnp.zeros(vpad_ref.shape, F32)
    kpad_ref[0:s_len, :] = k_new
    vpad_ref[0:s_len, :] = v_new

    rows = H_A * s_len
    ncol = wb + pad
    row_i = lax.broadcasted_iota(jnp.int32, (rows, ncol), 0)
    col_i = lax.broadcasted_iota(jnp.int32, (rows, ncol), 1)
    s_i = row_i % s_len
    d = wb + s_i - col_i
    mult = jnp.zeros((rows, ncol), F32)
    for win, dil in DILATED_CONFIGS:
        hit = (d >= 0) & (d <= win) & ((d % dil) == 0)
        mult = mult + jnp.where(hit, 1.0, 0.0)
    slope = jnp.zeros((rows, ncol), F32)
    for h in range(H_A):
        slope = jnp.where(row_i // s_len == h, ALIBI_SLOPES[h], slope)
    bias = slope * d.astype(F32)

    q = q_ref[...] * (HEAD_DIM ** -0.5)
    lane_head = lax.broadcasted_iota(jnp.int32, (s_len, D_A), 1) // HEAD_DIM
    qh = jnp.concatenate([jnp.where(lane_head == h, q, 0.0) for h in range(H_A)],
                         axis=0).astype(BF16)
    nt = (((1,), (1,)), ((), ()))
    s1 = lax.dot_general(qh, ck_ref[...].astype(BF16), nt, preferred_element_type=F32)
    s2 = lax.dot_general(qh, kpad_ref[...].astype(BF16), nt, preferred_element_type=F32)
    s = jnp.concatenate([s1, s2], axis=1) - bias
    s = jnp.where(mult > 0.0, s, -jnp.inf)
    m = jnp.max(s, axis=-1, keepdims=True)
    p = jnp.exp(s - m) * mult
    l = jnp.sum(p, axis=-1, keepdims=True)
    pb = p.astype(BF16)
    o = jnp.dot(pb[:, 0:wb], cv_ref[...].astype(BF16), preferred_element_type=F32)
    o = o + jnp.dot(pb[:, wb:ncol], vpad_ref[...].astype(BF16), preferred_element_type=F32)
    o = o / l
    out = jnp.zeros((s_len, D_A), F32)
    for h in range(H_A):
        out = jnp.where(lane_head == h, o[h * s_len:(h + 1) * s_len, :], out)
    out = _rms(out, g_ref[...])
    o0_ref[...] = out[:, 0:128]
    o1_ref[...] = out[:, 128:256]


def _attn_sample(proj, cache_k, cache_v, g, new_k, new_v, layer, batch, s_len, wb):
    depth = cache_k.shape[0]
    qb = COL_Q // D_A
    cache_spec = pl.BlockSpec((None, None, wb, D_A), lambda b: (layer, b, 0, 0))
    return pl.pallas_call(
        functools.partial(_attn_sample_kernel, wb=wb, s_len=s_len),
        grid=(batch,),
        in_specs=[
            pl.BlockSpec((s_len, D_A), lambda b: (b, qb)),
            pl.BlockSpec((s_len, D_A), lambda b: (b, qb + 1)),
            pl.BlockSpec((s_len, D_A), lambda b: (b, qb + 2)),
            cache_spec, cache_spec,
            pl.BlockSpec((None, 1, D_A), lambda b: (layer, 0, 0)),
            pl.BlockSpec(memory_space=pl.ANY), pl.BlockSpec(memory_space=pl.ANY),
        ],
        out_specs=[pl.BlockSpec((s_len, 128), lambda b: (b, 0))] * 2 + [cache_spec, cache_spec],
        out_shape=[jax.ShapeDtypeStruct((batch * s_len, 128), F32)] * 2 + [
                   jax.ShapeDtypeStruct((depth, batch, wb, D_A), F32)] * 2,
        input_output_aliases={6: 2, 7: 3},
        scratch_shapes=[pltpu.VMEM((128, D_A), F32)] * 2,
        compiler_params=pltpu.CompilerParams(
            dimension_semantics=("parallel",), vmem_limit_bytes=VMEM_LIMIT),
        name="attn_sample",
    )(proj, proj, proj, cache_k, cache_v, g, new_k, new_v)


def _ssd_kernel(xbc_ref, z_ref, gate_ref, dtr_ref, h0_ref, cprev_ref, sprev_ref,
                cw_ref, cb_ref, alog_ref, dtb_ref, dsk_ref, sn_ref, scw_ref, scn_ref,
                y_ref, hout_ref, cnew_ref, snew_ref,
                ext_ref, ext2_ref, h_ref, *, chunk, n_valid):
    c = pl.program_id(1)
    nc = pl.num_programs(1)
    C = chunk
    heads_per_group = H_B // SSM_GROUPS
    gw = heads_per_group * HEAD_DIM

    @pl.when(c == 0)
    def _init():
        h_ref[...] = h0_ref[...]
        ext_ref[0:8, :] = cprev_ref[...]
        ext2_ref[0:8, :] = sprev_ref[...]

    xbc = xbc_ref[...]
    ext_ref[8:8 + C, :] = xbc
    conv = xbc * cw_ref[CONV_W - 1:CONV_W, :]
    for k in range(CONV_W - 1):
        off = 8 - (CONV_W - 1) + k
        conv = conv + ext_ref[off:off + C, :] * cw_ref[k:k + 1, :]
    xc = _silu(conv + cb_ref[...])
    x_s = xc[:, 0:D_B]
    bm = xc[:, D_B:D_B + SSM_GROUPS * N_STATE]
    cm = xc[:, D_B + SSM_GROUPS * N_STATE:CONV_DIM]

    row = lax.broadcasted_iota(jnp.int32, (C, C), 0)
    col = lax.broadcasted_iota(jnp.int32, (C, C), 1)
    causal = row >= col
    dt = _softplus(dtr_ref[...] + dtb_ref[...])
    if n_valid < C:
        rvalid = lax.broadcasted_iota(jnp.int32, dt.shape, 0) < n_valid
        dt = jnp.where(rvalid, dt, 0.0)
    a = -jnp.exp(alog_ref[...])
    tri = jnp.where(causal, 1.0, 0.0)
    acs = jnp.dot(tri, dt * a, preferred_element_type=F32,
                  precision=lax.Precision.HIGHEST)
    acs_t = acs.T
    dt_t = dt.T
    e_acs = jnp.exp(acs)
    last = acs[C - 1:C, :]
    w_end = jnp.exp(last - acs) * dt
    cdec = jnp.exp(last)

    lane_head = lax.broadcasted_iota(jnp.int32, (C, D_B), 1) // HEAD_DIM
    e_acs_x = jnp.zeros((C, D_B), F32)
    w_end_x = jnp.zeros((C, D_B), F32)
    for h in range(H_B):
        sel = lane_head == h
        e_acs_x = jnp.where(sel, e_acs[:, h:h + 1], e_acs_x)
        w_end_x = jnp.where(sel, w_end[:, h:h + 1], w_end_x)

    nt = (((1,), (1,)), ((), ()))
    tn = (((0,), (0,)), ((), ()))
    x_bf = x_s.astype(BF16)
    xw = (x_s * w_end_x)
    y_parts = []
    lane_lo = lax.broadcasted_iota(jnp.int32, (C, 2 * HEAD_DIM), 1) < HEAD_DIM
    for g in range(SSM_GROUPS):
        bm_g = bm[:, g * N_STATE:(g + 1) * N_STATE].astype(BF16)
        cm_g = cm[:, g * N_STATE:(g + 1) * N_STATE].astype(BF16)
        scores = lax.dot_general(cm_g, bm_g, nt, preferred_element_type=F32)
        h_g = h_ref[g * heads_per_group:(g + 1) * heads_per_group].reshape(gw, N_STATE)
        y_off = lax.dot_general(cm_g, h_g.astype(BF16), nt, preferred_element_type=F32)
        y_off = y_off * e_acs_x[:, g * gw:(g + 1) * gw]
        diag = []
        for pair in range(heads_per_group // 2):
            outs = []
            for hh in range(2):
                h = g * heads_per_group + 2 * pair + hh
                seg = acs[:, h:h + 1] - acs_t[h:h + 1, :]
                decay = jnp.exp(jnp.where(causal, seg, -jnp.inf))
                mh = (scores * decay * dt_t[h:h + 1, :]).astype(BF16)
                lo = (g * heads_per_group + 2 * pair) * HEAD_DIM
                outs.append(jnp.dot(mh, x_bf[:, lo:lo + 2 * HEAD_DIM],
                                    preferred_element_type=F32))
            diag.append(jnp.where(lane_lo, outs[0], outs[1]))
        y_parts.append(jnp.concatenate(diag, axis=1) + y_off)
        st = lax.dot_general(xw[:, g * gw:(g + 1) * gw].astype(BF16), bm_g, tn,
                             preferred_element_type=F32)
        for hh in range(heads_per_group):
            h = g * heads_per_group + hh
            h_ref[h] = h_ref[h] * cdec[0:1, h:h + 1] + st[hh * HEAD_DIM:(hh + 1) * HEAD_DIM, :]
    y = jnp.concatenate(y_parts, axis=1) + dsk_ref[...] * x_s
    y_b = _rms(y * _silu(z_ref[...]), sn_ref[...])

    gates = gate_ref[...]
    b_gate = gates[:, 0:D_C]
    prod = gates[:, D_C:2 * D_C] * gates[:, 2 * D_C:3 * D_C]
    ext2_ref[8:8 + C, :] = prod
    sconv = prod * scw_ref[SCONV_W - 1:SCONV_W, :]
    for k in range(SCONV_W - 1):
        off = 8 - (SCONV_W - 1) + k
        sconv = sconv + ext2_ref[off:off + C, :] * scw_ref[k:k + 1, :]
    y_c = _rms(b_gate * sconv, scn_ref[...])

    y_ref[:, 0:D_B] = y_b.astype(y_ref.dtype)
    y_ref[:, D_B:D_B + D_C] = y_c.astype(y_ref.dtype)

    @pl.when(c == nc - 1)
    def _final():
        hout_ref[...] = h_ref[...]
        cnew_ref[...] = ext_ref[n_valid:n_valid + 8, :]
        snew_ref[...] = ext2_ref[n_valid:n_valid + 8, :]

    ext_ref[0:8, :] = ext_ref[C:C + 8, :]
    ext2_ref[0:8, :] = ext2_ref[C:C + 8, :]


def _ssd(proj, h0, cprev8, sprev8, wts, layer, batch, nc, chunk, n_valid):
    (cw, cb, alog, dtb, dsk, sn, scw, scn) = wts

    def tok(width, blk_idx):
        return pl.BlockSpec((chunk, width), lambda b, c: (b * nc + c, blk_idx))

    def per_layer(shape):
        return pl.BlockSpec((None,) + shape, lambda b, c: (layer,) + (0,) * len(shape))

    state_spec = pl.BlockSpec((None, H_B, HEAD_DIM, N_STATE), lambda b, c: (b, 0, 0, 0))
    conv_spec = pl.BlockSpec((None, 8, CONV_DIM), lambda b, c: (b, 0, 0))
    sconv_spec = pl.BlockSpec((None, 8, D_C), lambda b, c: (b, 0, 0))
    return pl.pallas_call(
        functools.partial(_ssd_kernel, chunk=chunk, n_valid=n_valid),
        grid=(batch, nc),
        in_specs=[
            tok(CONV_DIM, COL_XBC // CONV_DIM),
            tok(D_B, COL_Z // D_B),
            tok(3 * D_C, COL_G // (3 * D_C)),
            tok(128, COL_DT // 128),
            state_spec, conv_spec, sconv_spec,
            per_layer((CONV_W, CONV_DIM)), per_layer((1, CONV_DIM)),
            per_layer((1, 128)), per_layer((1, 128)), per_layer((1, D_B)),
            per_layer((1, D_B)), per_layer((SCONV_W, D_C)), per_layer((1, D_C)),
        ],
        out_specs=[
            pl.BlockSpec((chunk, D_B + D_C), lambda b, c: (b * nc + c, 0)),
            state_spec, conv_spec, sconv_spec,
        ],
        out_shape=[
            jax.ShapeDtypeStruct((batch * nc * chunk, D_B + D_C), BF16),
            jax.ShapeDtypeStruct((batch, H_B, HEAD_DIM, N_STATE), F32),
            jax.ShapeDtypeStruct((batch, 8, CONV_DIM), F32),
            jax.ShapeDtypeStruct((batch, 8, D_C), F32),
        ],
        scratch_shapes=[
            pltpu.VMEM((chunk + 8, CONV_DIM), F32),
            pltpu.VMEM((chunk + 8, D_C), F32),
            pltpu.VMEM((H_B, HEAD_DIM, N_STATE), F32),
        ],
        compiler_params=pltpu.CompilerParams(
            dimension_semantics=("parallel", "arbitrary"), vmem_limit_bytes=VMEM_LIMIT),
        name="ssd_sconv",
    )(proj, proj, proj, proj, h0, cprev8, sprev8, cw, cb, alog, dtb, dsk, sn, scw, scn)


def _post_kernel(x_ref, oa0_ref, oa1_ref, ybc_ref, woa_ref, wobc_ref, gpost_ref, gpre_ref,
                 gmlp_ref, wup_ref, wdn_ref, o_ref, x1_ref, h_ref, acc_ref):
    j = pl.program_id(1)
    nj = pl.num_programs(1)

    @pl.when(j == 0)
    def _start():
        oa = jnp.concatenate([oa0_ref[...], oa1_ref[...]], axis=1).astype(BF16)
        mix = jnp.dot(oa, woa_ref[...], preferred_element_type=F32)
        mix = mix + jnp.dot(ybc_ref[...], wobc_ref[...], preferred_element_type=F32)
        x1 = x_ref[...] + _rms(mix, gpost_ref[...])
        x1_ref[...] = x1
        h_ref[...] = _rms(x1, gpre_ref[...]).astype(BF16)
        acc_ref[...] = jnp.zeros(acc_ref.shape, F32)

    u = jnp.dot(h_ref[...], wup_ref[...], preferred_element_type=F32)
    u = jnp.square(jnp.maximum(u, 0.0)).astype(BF16)
    acc_ref[...] += jnp.dot(u, wdn_ref[...], preferred_element_type=F32)

    @pl.when(j == nj - 1)
    def _finish():
        o_ref[...] = x1_ref[...] + _rms(acc_ref[...], gmlp_ref[...])


def _post(x2d, oa0, oa1, ybc, w_out_a, w_out_bc, gpost, gpre, gmlp, w_up, w_dn, layer, tm, tf):
    m = x2d.shape[0]

    def gain():
        return pl.BlockSpec((None, 1, D_MODEL), lambda i, j: (layer, 0, 0))

    return pl.pallas_call(
        _post_kernel,
        grid=(m // tm, D_FF // tf),
        in_specs=[
            pl.BlockSpec((tm, D_MODEL), lambda i, j: (i, 0)),
            pl.BlockSpec((tm, 128), lambda i, j: (i, 0)),
            pl.BlockSpec((tm, 128), lambda i, j: (i, 0)),
            pl.BlockSpec((tm, D_B + D_C), lambda i, j: (i, 0)),
            pl.BlockSpec((None, D_A, D_MODEL), lambda i, j: (layer, 0, 0)),
            pl.BlockSpec((None, D_B + D_C, D_MODEL), lambda i, j: (layer, 0, 0)),
            gain(), gain(), gain(),
            pl.BlockSpec((None, D_MODEL, tf), lambda i, j: (layer, 0, j)),
            pl.BlockSpec((None, tf, D_MODEL), lambda i, j: (layer, j, 0)),
        ],
        out_specs=pl.BlockSpec((tm, D_MODEL), lambda i, j: (i, 0)),
        out_shape=jax.ShapeDtypeStruct((m, D_MODEL), F32),
        scratch_shapes=[
            pltpu.VMEM((tm, D_MODEL), F32),
            pltpu.VMEM((tm, D_MODEL), BF16),
            pltpu.VMEM((tm, D_MODEL), F32),
        ],
        compiler_params=pltpu.CompilerParams(
            dimension_semantics=("parallel", "arbitrary"), vmem_limit_bytes=VMEM_LIMIT),
        name="post",
    )(x2d, oa0, oa1, ybc, w_out_a, w_out_bc, gpost, gpre, gmlp, w_up, w_dn)


def _prep_weights(norm_mix_pre, norm_mix_post, norm_mlp_pre, norm_mlp_post, w_in, w_out,
                  attn_norm, ssm_conv_w, ssm_conv_b, ssm_a_log, ssm_dt_bias, ssm_d, ssm_norm,
                  sconv_w, sconv_norm, w_mlp_up, w_mlp_down):
    depth = w_in.shape[0]
    o = 0
    cols = {}
    for name, width in (("q", D_A), ("k", D_A), ("v", D_A), ("z", D_B), ("xbc", CONV_DIM),
                        ("dt", H_B), ("b", D_C), ("c", D_C), ("u", D_C)):
        cols[name] = (o, o + width)
        o += width

    def cs(name):
        lo, hi = cols[name]
        return w_in[:, :, lo:hi]

    pad_dt = jnp.zeros((depth, D_MODEL, 128 - H_B), w_in.dtype)
    w_in_r = jnp.concatenate(
        [cs("xbc"), cs("z"), cs("q"), cs("k"), cs("v"), cs("b"), cs("c"), cs("u"),
         cs("dt"), pad_dt], axis=2).astype(BF16)

    def lane_pad(v):
        return jnp.pad(v.astype(F32), ((0, 0), (0, 128 - H_B)))[:, None, :]

    def row(v):
        return v.astype(F32)[:, None, :]

    return dict(
        g_mix_pre=row(norm_mix_pre), g_mix_post=row(norm_mix_post),
        g_mlp_pre=row(norm_mlp_pre), g_mlp_post=row(norm_mlp_post),
        w_in=w_in_r,
        w_out_a=w_out[:, 0:D_A, :].astype(BF16),
        w_out_bc=w_out[:, D_A:, :].astype(BF16),
        attn_norm=row(attn_norm),
        ssd=(ssm_conv_w.astype(F32), row(ssm_conv_b), lane_pad(ssm_a_log),
             lane_pad(ssm_dt_bias), row(jnp.repeat(ssm_d, HEAD_DIM, axis=1)),
             row(ssm_norm), sconv_w.astype(F32), row(sconv_norm)),
        w_up=w_mlp_up.astype(BF16), w_dn=w_mlp_down.astype(BF16),
    )


def _run_trunk(x, states, w, prompt):
    batch, seq, _ = x.shape
    depth = w["w_in"].shape[0]
    m = batch * seq
    x2d = x.reshape(m, D_MODEL)
    tm = 512 if m % 512 == 0 else m
    outs = [[] for _ in range(5)]
    if prompt:
        chunk, nc = SSD_CHUNK, seq // SSD_CHUNK
        wb = min(WIN_MAX, seq)
    else:
        chunk, nc = seq, 1
        cache_k, cache_v, st_ssm, st_conv, st_sconv = states
        wb = cache_k.shape[2]
        cache_k = cache_k.reshape(depth, batch, wb, D_A)
        cache_v = cache_v.reshape(depth, batch, wb, D_A)
        new_k = jnp.zeros((depth, batch, wb, D_A), F32)
        new_v = jnp.zeros((depth, batch, wb, D_A), F32)
    for l in range(depth):
        if prompt:
            proj, qkv_perm = _in_proj(x2d, w["g_mix_pre"], w["w_in"], l, tm, seq)
            oa0, oa1 = _attn_prompt(qkv_perm, w["attn_norm"], l, batch, seq)
            p3 = proj.reshape(batch, seq, PROJ_W)
            new_k = p3[:, seq - wb:, COL_Q + D_A:COL_Q + 2 * D_A]
            new_v = p3[:, seq - wb:, COL_Q + 2 * D_A:COL_Q + 3 * D_A]
            h0 = jnp.zeros((batch, H_B, HEAD_DIM, N_STATE), F32)
            cprev8 = jnp.zeros((batch, 8, CONV_DIM), F32)
            sprev8 = jnp.zeros((batch, 8, D_C), F32)
        else:
            (proj,) = _in_proj(x2d, w["g_mix_pre"], w["w_in"], l, tm)
            oa0, oa1, new_k, new_v = _attn_sample(proj, cache_k, cache_v, w["attn_norm"],
                                                  new_k, new_v, l, batch, seq, wb)
            h0 = st_ssm[l].astype(F32)
            cprev8 = jnp.pad(st_conv[l].astype(F32), ((0, 0), (8 - (CONV_W - 1), 0), (0, 0)))
            sprev8 = jnp.pad(st_sconv[l].astype(F32), ((0, 0), (8 - (SCONV_W - 1), 0), (0, 0)))
        ybc, h_new, c8, s8 = _ssd(proj, h0, cprev8, sprev8, w["ssd"], l, batch, nc,
                                  chunk, chunk)
        x2d = _post(x2d, oa0, oa1, ybc, w["w_out_a"], w["w_out_bc"], w["g_mix_post"],
                    w["g_mlp_pre"], w["g_mlp_post"], w["w_up"], w["w_dn"], l, tm, 1024)
        if prompt:
            outs[0].append(new_k.reshape(batch, wb, H_A, HEAD_DIM))
            outs[1].append(new_v.reshape(batch, wb, H_A, HEAD_DIM))
        outs[2].append(h_new)
        outs[3].append(c8[:, 8 - (CONV_W - 1):, :])
        outs[4].append(s8[:, 8 - (SCONV_W - 1):, :])
    if not prompt:
        outs[0] = new_k.reshape(depth, batch, wb, H_A, HEAD_DIM)
        outs[1] = new_v.reshape(depth, batch, wb, H_A, HEAD_DIM)
    stacked = tuple(o if not isinstance(o, list) else jnp.stack(o) for o in outs)
    return x2d.reshape(batch, seq, D_MODEL), stacked


def kernel(x_prompt, x_sample, cache_attn_k, cache_attn_v, state_ssm, state_ssm_conv, state_sconv, norm_mix_pre, norm_mix_post, norm_mlp_pre, norm_mlp_post, w_in, w_out, attn_norm, ssm_conv_w, ssm_conv_b, ssm_a_log, ssm_dt_bias, ssm_d, ssm_norm, sconv_w, sconv_norm, w_mlp_up, w_mlp_down):
    w = _prep_weights(norm_mix_pre, norm_mix_post, norm_mlp_pre, norm_mlp_post, w_in, w_out,
                      attn_norm, ssm_conv_w, ssm_conv_b, ssm_a_log, ssm_dt_bias, ssm_d,
                      ssm_norm, sconv_w, sconv_norm, w_mlp_up, w_mlp_down)
    y_prompt, (p_k, p_v, p_ssm, p_conv, p_sconv) = _run_trunk(x_prompt, None, w, True)
    states = (cache_attn_k, cache_attn_v, state_ssm, state_ssm_conv, state_sconv)
    y_sample, (s_k, s_v, s_ssm, s_conv, s_sconv) = _run_trunk(x_sample, states, w, False)
    return (y_prompt, y_sample, p_k, p_v, p_ssm, p_conv, p_sconv,
            s_k, s_v, s_ssm, s_conv, s_sconv)
```

```python
import functools
import math

import jax
import jax.numpy as jnp
from jax import lax
from jax.experimental import pallas as pl
from jax.experimental.pallas import tpu as pltpu

F32 = jnp.float32
BF16 = jnp.bfloat16

HEAD_DIM = 64
D_MODEL = 1024
D_A = 256
H_A = 4
D_B = 512
H_B = 8
SSM_GROUPS = 2
N_STATE = 128
CONV_W = 4
CONV_DIM = D_B + 2 * SSM_GROUPS * N_STATE
D_C = 256
SCONV_W = 3
D_FF = 4 * D_MODEL
DILATED_CONFIGS = ((128, 1), (512, 4), (2048, 16))
WIN_MAX = 2048
EPS = 1e-6
LOG2E = 1.4426950408889634
ALIBI_SLOPES = tuple(2.0 ** (-8.0 * (h + 1) / H_A) for h in range(H_A))
ATTN_BLOCK = 128
ATTN_PERM = 16
ATTN_BLOCKS_PER_STEP = 8
SSD_CHUNK = 128
SSD_GROUP = 2

COL_XBC = 0
COL_Z = 1024
COL_Q = 1536
COL_G = 2304
COL_DT = 3072
PROJ_W = 3200

VMEM_LIMIT = 56 * 1024 * 1024


def _rms(x, g):
    return x * lax.rsqrt(jnp.mean(x * x, axis=-1, keepdims=True) + EPS) * g


def _silu(x):
    return x * (1.0 / (1.0 + jnp.exp(-x)))


def _softplus(x):
    return jnp.maximum(x, 0.0) + jnp.log1p(jnp.exp(-jnp.abs(x)))


def _in_proj_kernel(x_ref, g_ref, w_ref, o_ref, *rest, n_split, permute):
    h = _rms(x_ref[...], g_ref[...]).astype(BF16)
    tn = w_ref.shape[1] // n_split
    for j in range(n_split):
        o_ref[:, j * tn:(j + 1) * tn] = jnp.dot(
            h, w_ref[:, j * tn:(j + 1) * tn], preferred_element_type=F32)
    if permute:
        operm_ref, scr_ref = rest
        tm = x_ref.shape[0]
        for j in range(3 * D_A // 128):
            scr_ref[j] = o_ref[:, COL_Q + j * 128:COL_Q + (j + 1) * 128]
        for r in range(ATTN_PERM):
            for j in range(3 * D_A // 128):
                operm_ref[r, :, j * 128:(j + 1) * 128] = (
                    scr_ref.at[j][pl.ds(r, tm // ATTN_PERM, stride=ATTN_PERM), :])


def _in_proj(x2d, g, w, layer, tm, seq=None):
    m = x2d.shape[0]
    permute = seq is not None
    out_specs = [pl.BlockSpec((tm, PROJ_W), lambda i: (i, 0))]
    out_shape = [jax.ShapeDtypeStruct((m, PROJ_W), F32)]
    scratch = []
    if permute:
        tiles = seq // tm
        rows = tm // ATTN_PERM
        out_specs.append(pl.BlockSpec((None, ATTN_PERM, rows, 3 * D_A),
                                      lambda i: (i // tiles, 0, i % tiles, 0)))
        out_shape.append(jax.ShapeDtypeStruct((m // seq, ATTN_PERM, seq // ATTN_PERM, 3 * D_A), F32))
        scratch.append(pltpu.VMEM((3 * D_A // 128, tm, 128), F32))
    return pl.pallas_call(
        functools.partial(_in_proj_kernel, n_split=5, permute=permute),
        grid=(m // tm,),
        in_specs=[
            pl.BlockSpec((tm, D_MODEL), lambda i: (i, 0)),
            pl.BlockSpec((None, 1, D_MODEL), lambda i: (layer, 0, 0)),
            pl.BlockSpec((None, D_MODEL, PROJ_W), lambda i: (layer, 0, 0)),
        ],
        out_specs=out_specs,
        out_shape=out_shape,
        scratch_shapes=scratch,
        compiler_params=pltpu.CompilerParams(
            dimension_semantics=("parallel",), vmem_limit_bytes=VMEM_LIMIT),
        name="in_proj",
    )(x2d, g, w)


def _attn_prompt_kernel(q_ref, k_ref, v_ref, g_ref, o0_ref, o1_ref,
                        acc_ref, m_ref, l_ref, bias_ref, *, seq):
    blk = ATTN_BLOCK
    nbr = len(DILATED_CONFIGS)
    P = ATTN_PERM
    per_res = seq // P

    def local_to_strided(a, dil):
        nchunk = P // dil
        rows = blk // nchunk
        return nchunk * (a % rows) + a // rows

    shape4 = (H_A * blk, 2 * blk)
    row4 = lax.broadcasted_iota(jnp.int32, shape4, 0)
    ki = lax.broadcasted_iota(jnp.int32, shape4, 1)
    slope = jnp.zeros(shape4, F32)
    for h in range(H_A):
        slope = jnp.where(row4 // blk == h, ALIBI_SLOPES[h], slope)
    for bi, (win, dil) in enumerate(DILATED_CONFIGS):
        assert win // dil == blk and P % dil == 0
        jq = blk + local_to_strided(row4 % blk, dil)
        jk = local_to_strided(ki % blk, dil) + blk * (ki // blk)
        diff = jq - jk
        band = (diff >= 0) & (diff <= blk)
        dist = slope * (-float(dil) * LOG2E * diff.astype(F32))
        bias_ref[bi] = jnp.where(band, dist, -jnp.inf)
        bias_ref[nbr + bi] = jnp.where(band & (ki >= blk), dist, -jnp.inf)

    lane_head = lax.broadcasted_iota(jnp.int32, (blk, D_A), 1) // HEAD_DIM
    lane_lo = lax.broadcasted_iota(jnp.int32, (blk, 128), 1) < HEAD_DIM

    def per_head(col):
        c = [col[h * blk:(h + 1) * blk, :] for h in range(H_A)]
        return jnp.concatenate([jnp.where(lane_lo, c[0], c[1]), jnp.where(lane_lo, c[2], c[3])],
                               axis=1)

    def blocks(r, u, bi, dil):
        nchunk = P // dil
        rows = blk // nchunk
        nb = min(ATTN_BLOCKS_PER_STEP, per_res // rows)
        starts = [pl.multiple_of(jnp.maximum(nb * u - 1 + j, 0) * rows, rows)
                  for j in range(nb + 1)]

        def load(ref, st):
            return jnp.concatenate(
                [ref[dil * c + r, pl.ds(st, rows), :] for c in range(nchunk)], axis=0)

        def store(ref, st, val):
            for c in range(nchunk):
                ref[dil * c + r, pl.ds(st, rows), :] = val[c * rows:(c + 1) * rows, :]

        kb = [load(k_ref, st).astype(BF16) for st in starts]
        vb = [load(v_ref, st).astype(BF16) for st in starts]
        results = []
        for j in range(nb):
            q = (load(q_ref, starts[1 + j]) * (HEAD_DIM ** -0.5 * LOG2E)).astype(BF16)
            q4 = jnp.concatenate(
                [jnp.where(lane_head == h, q, jnp.zeros_like(q)) for h in range(H_A)], axis=0)
            kk = jnp.concatenate([kb[j], kb[j + 1]], axis=0)
            vv = jnp.concatenate([vb[j], vb[j + 1]], axis=0)
            s = lax.dot_general(q4, kk, (((1,), (1,)), ((), ())), preferred_element_type=F32)
            if j == 0:
                s = s + bias_ref[jnp.where(u == 0, nbr + bi, bi)]
            else:
                s = s + bias_ref[bi]
            m4 = jnp.max(s, axis=-1, keepdims=True)
            p = jnp.exp2(s - m4)
            l4 = jnp.sum(p, axis=-1, keepdims=True)
            pb = p.astype(BF16)
            o = [jnp.dot(pb[h * blk:(h + 1) * blk, :],
                         vv[:, (h // 2) * 128:(h // 2 + 1) * 128],
                         preferred_element_type=F32) for h in range(H_A)]
            acc_b = jnp.concatenate([jnp.where(lane_lo, o[0], o[1]),
                                     jnp.where(lane_lo, o[2], o[3])], axis=1)
            m_b = per_head(m4)
            l_b = per_head(l4)
            if bi > 0:
                st = starts[1 + j]
                m_old = load(m_ref, st)
                m_new = jnp.maximum(m_old, m_b)
                a_old = jnp.exp2(m_old - m_new)
                a_b = jnp.exp2(m_b - m_new)
                acc_b = load(acc_ref, st) * a_old + acc_b * a_b
                l_b = load(l_ref, st) * a_old + l_b * a_b
                m_b = m_new
            results.append((acc_b, m_b, l_b))
        for j in range(nb):
            store(acc_ref, starts[1 + j], results[j][0])
            store(m_ref, starts[1 + j], results[j][1])
            store(l_ref, starts[1 + j], results[j][2])

    for bi, (win, dil) in enumerate(DILATED_CONFIGS):
        nblk = per_res // (blk // (P // dil))
        nb = min(ATTN_BLOCKS_PER_STEP, nblk)
        nstep = nblk // nb
        nres = ATTN_BLOCKS_PER_STEP // nb
        assert dil % nres == 0

        def body(it, carry, bi=bi, dil=dil, nstep=nstep, nres=nres):
            rg = it // nstep
            u = it - rg * nstep
            for rr in range(nres):
                blocks(rg * nres + rr, u, bi, dil)
            return carry

        lax.fori_loop(0, (dil // nres) * nstep, body, 0)

    def finish(r, carry):
        o = _rms(acc_ref[r] / l_ref[r], g_ref[...])
        o0_ref[pl.ds(r, per_res, stride=P), :] = o[:, 0:128]
        o1_ref[pl.ds(r, per_res, stride=P), :] = o[:, 128:256]
        return carry

    lax.fori_loop(0, P, finish, 0)


def _attn_prompt(qkv_perm, g, layer, batch, seq):
    per_res = seq // ATTN_PERM
    qkv_specs = [pl.BlockSpec((None, ATTN_PERM, per_res, D_A),
                              functools.partial(lambda b, j: (b, 0, 0, j), j=j)) for j in range(3)]
    half_spec = pl.BlockSpec((seq, 128), lambda b: (b, 0))
    return pl.pallas_call(
        functools.partial(_attn_prompt_kernel, seq=seq),
        grid=(batch,),
        in_specs=qkv_specs + [pl.BlockSpec((None, 1, D_A), lambda b: (layer, 0, 0))],
        out_specs=[half_spec, half_spec],
        out_shape=[jax.ShapeDtypeStruct((batch * seq, 128), F32)] * 2,
        scratch_shapes=[pltpu.VMEM((ATTN_PERM, per_res, D_A), F32)] * 3
        + [pltpu.VMEM((2 * len(DILATED_CONFIGS), H_A * ATTN_BLOCK, 2 * ATTN_BLOCK), F32)],
        compiler_params=pltpu.CompilerParams(
            dimension_semantics=("parallel",), vmem_limit_bytes=VMEM_LIMIT),
        name="attn_prompt",
    )(qkv_perm, qkv_perm, qkv_perm, g)


def _attn_sample_kernel(q_ref, k_ref, v_ref, ckt_ref, cvt_ref, g_ref, nk_in_ref, nv_in_ref,
                        o0_ref, o1_ref, nkt_ref, nvt_ref, kpad_ref, vpad_ref, *, wb, s_len):
    pad = 128
    k_new = k_ref[...]
    v_new = v_ref[...]
    kpad_ref[...] = jnp.zeros(kpad_ref.shape, F32)
    vpad_ref[...] = jnp.zeros(vpad_ref.shape, F32)
    kpad_ref[pad - s_len:pad, :] = k_new
    vpad_ref[pad - s_len:pad, :] = v_new
    kpad = kpad_ref[...]
    vpad = vpad_ref[...]
    kpad_t = kpad.T
    vpad_t = vpad.T

    tail = lax.broadcasted_iota(jnp.int32, (D_A, pad), 1) >= pad - s_len
    for src_ref, dst_ref, new_t in ((ckt_ref, nkt_ref, kpad_t), (cvt_ref, nvt_ref, vpad_t)):
        rolled = pltpu.roll(src_ref[...], wb - s_len, axis=1)
        dst_ref[:, 0:wb - pad] = rolled[:, 0:wb - pad]
        dst_ref[:, wb - pad:wb] = jnp.where(tail, new_t, rolled[:, wb - pad:wb])

    rows = H_A * s_len
    ncol = wb + pad
    row_i = lax.broadcasted_iota(jnp.int32, (rows, ncol), 0)
    col_i = lax.broadcasted_iota(jnp.int32, (rows, ncol), 1)
    s_i = row_i % s_len
    pos = jnp.where(col_i < wb, col_i, col_i + s_len - pad)
    d = wb + s_i - pos
    real = (col_i < wb) | (col_i >= ncol - s_len)
    mult = jnp.zeros((rows, ncol), F32)
    for win, dil in DILATED_CONFIGS:
        hit = real & (d >= 0) & (d <= win) & ((d % dil) == 0)
        mult = mult + jnp.where(hit, 1.0, 0.0)
    slope = jnp.zeros((rows, ncol), F32)
    for h in range(H_A):
        slope = jnp.where(row_i // s_len == h, ALIBI_SLOPES[h], slope)
    bias = slope * d.astype(F32)

    q = q_ref[...] * (HEAD_DIM ** -0.5)
    lane_head = lax.broadcasted_iota(jnp.int32, (s_len, D_A), 1) // HEAD_DIM
    qh = jnp.concatenate([jnp.where(lane_head == h, q, 0.0) for h in range(H_A)],
                         axis=0).astype(BF16)
    s1 = jnp.dot(qh, ckt_ref[...].astype(BF16), preferred_element_type=F32)
    s2 = jnp.dot(qh, kpad_t.astype(BF16), preferred_element_type=F32)
    s = jnp.concatenate([s1, s2], axis=1) - bias
    s = jnp.where(mult > 0.0, s, -jnp.inf)
    m = jnp.max(s, axis=-1, keepdims=True)
    p = jnp.exp(s - m) * mult
    l = jnp.sum(p, axis=-1, keepdims=True)
    pb = p.astype(BF16)
    nt = (((1,), (1,)), ((), ()))
    o = lax.dot_general(pb[:, 0:wb], cvt_ref[...].astype(BF16), nt, preferred_element_type=F32)
    o = o + jnp.dot(pb[:, wb:ncol], vpad.astype(BF16), preferred_element_type=F32)
    o = o / l
    out = jnp.zeros((s_len, D_A), F32)
    for h in range(H_A):
        out = jnp.where(lane_head == h, o[h * s_len:(h + 1) * s_len, :], out)
    out = _rms(out, g_ref[...])
    o0_ref[...] = out[:, 0:128]
    o1_ref[...] = out[:, 128:256]


def _attn_sample(proj, cache_k, cache_v, g, new_k, new_v, layer, batch, s_len, wb):
    depth = cache_k.shape[0]
    qb = COL_Q // D_A
    cache_spec = pl.BlockSpec((None, None, D_A, wb), lambda b: (layer, b, 0, 0))
    return pl.pallas_call(
        functools.partial(_attn_sample_kernel, wb=wb, s_len=s_len),
        grid=(batch,),
        in_specs=[
            pl.BlockSpec((s_len, D_A), lambda b: (b, qb)),
            pl.BlockSpec((s_len, D_A), lambda b: (b, qb + 1)),
            pl.BlockSpec((s_len, D_A), lambda b: (b, qb + 2)),
            cache_spec, cache_spec,
            pl.BlockSpec((None, 1, D_A), lambda b: (layer, 0, 0)),
            pl.BlockSpec(memory_space=pl.ANY), pl.BlockSpec(memory_space=pl.ANY),
        ],
        out_specs=[pl.BlockSpec((s_len, 128), lambda b: (b, 0))] * 2 + [cache_spec, cache_spec],
        out_shape=[jax.ShapeDtypeStruct((batch * s_len, 128), F32)] * 2 + [
                   jax.ShapeDtypeStruct((depth, batch, D_A, wb), F32)] * 2,
        input_output_aliases={6: 2, 7: 3},
        scratch_shapes=[pltpu.VMEM((128, D_A), F32)] * 2,
        compiler_params=pltpu.CompilerParams(
            dimension_semantics=("parallel",), vmem_limit_bytes=VMEM_LIMIT),
        name="attn_sample",
    )(proj, proj, proj, cache_k, cache_v, g, new_k, new_v)


def _ssd_kernel(xbc_ref, z_ref, gate_ref, dtr_ref, h0_ref, cprev_ref, sprev_ref,
                cw_ref, cb_ref, alog_ref, dtb_ref, dsk_ref, sn_ref, scw_ref, scn_ref,
                y_ref, hout_ref, cnew_ref, snew_ref,
                ext_ref, ext2_ref, h_ref, *, chunk, group):
    c = pl.program_id(1)
    nc = pl.num_programs(1)

    @pl.when(c == 0)
    def _init():
        h_ref[...] = h0_ref[...]
        ext_ref[:, 0:8, :] = cprev_ref[...]
        ext2_ref[:, 0:8, :] = sprev_ref[...]

    for i in range(group):
        _ssd_one(xbc_ref.at[i], z_ref.at[i], gate_ref.at[i], dtr_ref.at[i],
                 cw_ref, cb_ref, alog_ref, dtb_ref, dsk_ref, sn_ref, scw_ref, scn_ref,
                 y_ref.at[i], ext_ref.at[i], ext2_ref.at[i], h_ref.at[i], chunk=chunk)

    @pl.when(c == nc - 1)
    def _final():
        hout_ref[...] = h_ref[...]
        cnew_ref[...] = ext_ref[:, chunk:chunk + 8, :]
        snew_ref[...] = ext2_ref[:, chunk:chunk + 8, :]

    ext_ref[:, 0:8, :] = ext_ref[:, chunk:chunk + 8, :]
    ext2_ref[:, 0:8, :] = ext2_ref[:, chunk:chunk + 8, :]


def _ssd_one(xbc_ref, z_ref, gate_ref, dtr_ref,
             cw_ref, cb_ref, alog_ref, dtb_ref, dsk_ref, sn_ref, scw_ref, scn_ref,
             y_ref, ext_ref, ext2_ref, h_ref, *, chunk):
    C = chunk
    heads_per_group = H_B // SSM_GROUPS
    gw = heads_per_group * HEAD_DIM

    xbc = xbc_ref[...]
    ext_ref[8:8 + C, :] = xbc
    conv = xbc * cw_ref[CONV_W - 1:CONV_W, :]
    for k in range(CONV_W - 1):
        off = 8 - (CONV_W - 1) + k
        conv = conv + ext_ref[off:off + C, :] * cw_ref[k:k + 1, :]
    xc = _silu(conv + cb_ref[...])
    x_s = xc[:, 0:D_B]
    bm = xc[:, D_B:D_B + SSM_GROUPS * N_STATE]
    cm = xc[:, D_B + SSM_GROUPS * N_STATE:CONV_DIM]

    row = lax.broadcasted_iota(jnp.int32, (C, C), 0)
    col = lax.broadcasted_iota(jnp.int32, (C, C), 1)
    causal = row >= col
    dt8 = _softplus((dtr_ref[...] + dtb_ref[...]).T[0:H_B, :])
    a8 = jnp.broadcast_to(-jnp.exp(alog_ref[...]), (128, 128)).T[0:H_B, 0:C]
    da = dt8 * a8
    upper = jnp.where(row <= col, 1.0, 0.0).astype(BF16)
    da_hi = da.astype(BF16)
    rem = da - da_hi.astype(F32)
    da_mid = rem.astype(BF16)
    da_lo = (rem - da_mid.astype(F32)).astype(BF16)
    acs8 = (jnp.dot(da_hi, upper, preferred_element_type=F32)
            + jnp.dot(da_mid, upper, preferred_element_type=F32)
            + jnp.dot(da_lo, upper, preferred_element_type=F32))
    last8 = acs8[:, C - 1:C]
    cdec8 = jnp.exp(last8)
    stacked = jnp.concatenate(
        [acs8, jnp.exp(acs8), jnp.exp(last8 - acs8) * dt8, jnp.zeros((128 - 3 * H_B, C), F32)],
        axis=0)
    cols = stacked.T
    lane_lo = lax.broadcasted_iota(jnp.int32, (C, 2 * HEAD_DIM), 1) < HEAD_DIM

    def expand(base):
        return jnp.concatenate(
            [jnp.where(lane_lo, cols[:, base + 2 * p:base + 2 * p + 1],
                       cols[:, base + 2 * p + 1:base + 2 * p + 2]) for p in range(H_B // 2)],
            axis=1)

    e_acs_x = expand(H_B)
    w_end_x = expand(2 * H_B)

    nt = (((1,), (1,)), ((), ()))
    tn = (((0,), (0,)), ((), ()))
    x_bf = x_s.astype(BF16)
    xw = (x_s * w_end_x)
    y_parts = []
    for g in range(SSM_GROUPS):
        bm_g = bm[:, g * N_STATE:(g + 1) * N_STATE].astype(BF16)
        cm_g = cm[:, g * N_STATE:(g + 1) * N_STATE].astype(BF16)
        scores = lax.dot_general(cm_g, bm_g, nt, preferred_element_type=F32)
        h_g = h_ref[g * heads_per_group:(g + 1) * heads_per_group].reshape(gw, N_STATE)
        y_off = lax.dot_general(cm_g, h_g.astype(BF16), nt, preferred_element_type=F32)
        y_off = y_off * e_acs_x[:, g * gw:(g + 1) * gw]
        diag = []
        for pair in range(heads_per_group // 2):
            outs = []
            for hh in range(2):
                h = g * heads_per_group + 2 * pair + hh
                seg = cols[:, h:h + 1] - acs8[h:h + 1, :]
                decay = jnp.exp(jnp.where(causal, seg, -jnp.inf))
                mh = (scores * decay * dt8[h:h + 1, :]).astype(BF16)
                lo = (g * heads_per_group + 2 * pair) * HEAD_DIM
                outs.append(jnp.dot(mh, x_bf[:, lo:lo + 2 * HEAD_DIM],
                                    preferred_element_type=F32))
            diag.append(jnp.where(lane_lo, outs[0], outs[1]))
        y_parts.append(jnp.concatenate(diag, axis=1) + y_off)
        st = lax.dot_general(xw[:, g * gw:(g + 1) * gw].astype(BF16), bm_g, tn,
                             preferred_element_type=F32)
        for hh in range(heads_per_group):
            h = g * heads_per_group + hh
            h_ref[h] = h_ref[h] * cdec8[h:h + 1, 0:1] + st[hh * HEAD_DIM:(hh + 1) * HEAD_DIM, :]
    y = jnp.concatenate(y_parts, axis=1) + dsk_ref[...] * x_s
    y_b = _rms(y * _silu(z_ref[...]), sn_ref[...])

    gates = gate_ref[...]
    b_gate = gates[:, 0:D_C]
    prod = gates[:, D_C:2 * D_C] * gates[:, 2 * D_C:3 * D_C]
    ext2_ref[8:8 + C, :] = prod
    sconv = prod * scw_ref[SCONV_W - 1:SCONV_W, :]
    for k in range(SCONV_W - 1):
        off = 8 - (SCONV_W - 1) + k
        sconv = sconv + ext2_ref[off:off + C, :] * scw_ref[k:k + 1, :]
    y_c = _rms(b_gate * sconv, scn_ref[...])

    y_ref[:, 0:D_B] = y_b.astype(y_ref.dtype)
    y_ref[:, D_B:D_B + D_C] = y_c.astype(y_ref.dtype)


def _ssd(proj, h0, cprev8, sprev8, wts, layer, batch, nc, chunk, group):
    (cw, cb, alog, dtb, dsk, sn, scw, scn) = wts
    proj3 = proj.reshape(batch, nc * chunk, PROJ_W)

    def tok(width, blk_idx):
        return pl.BlockSpec((group, chunk, width), lambda b, c: (b, c, blk_idx))

    def per_layer(shape):
        return pl.BlockSpec((None,) + shape, lambda b, c: (layer,) + (0,) * len(shape))

    state_spec = pl.BlockSpec((group, H_B, HEAD_DIM, N_STATE), lambda b, c: (b, 0, 0, 0))
    conv_spec = pl.BlockSpec((group, 8, CONV_DIM), lambda b, c: (b, 0, 0))
    sconv_spec = pl.BlockSpec((group, 8, D_C), lambda b, c: (b, 0, 0))
    return pl.pallas_call(
        functools.partial(_ssd_kernel, chunk=chunk, group=group),
        grid=(batch // group, nc),
        in_specs=[
            tok(CONV_DIM, COL_XBC // CONV_DIM),
            tok(D_B, COL_Z // D_B),
            tok(3 * D_C, COL_G // (3 * D_C)),
            tok(128, COL_DT // 128),
            state_spec, conv_spec, sconv_spec,
            per_layer((CONV_W, CONV_DIM)), per_layer((1, CONV_DIM)),
            per_layer((1, 128)), per_layer((1, 128)), per_layer((1, D_B)),
            per_layer((1, D_B)), per_layer((SCONV_W, D_C)), per_layer((1, D_C)),
        ],
        out_specs=[
            pl.BlockSpec((group, chunk, D_B + D_C), lambda b, c: (b, c, 0)),
            state_spec, conv_spec, sconv_spec,
        ],
        out_shape=[
            jax.ShapeDtypeStruct((batch, nc * chunk, D_B + D_C), BF16),
            jax.ShapeDtypeStruct((batch, H_B, HEAD_DIM, N_STATE), F32),
            jax.ShapeDtypeStruct((batch, 8, CONV_DIM), F32),
            jax.ShapeDtypeStruct((batch, 8, D_C), F32),
        ],
        scratch_shapes=[
            pltpu.VMEM((group, chunk + 8, CONV_DIM), F32),
            pltpu.VMEM((group, chunk + 8, D_C), F32),
            pltpu.VMEM((group, H_B, HEAD_DIM, N_STATE), F32),
        ],
        compiler_params=pltpu.CompilerParams(
            dimension_semantics=("parallel", "arbitrary"), vmem_limit_bytes=VMEM_LIMIT),
        name="ssd_sconv",
    )(proj3, proj3, proj3, proj3, h0, cprev8, sprev8, cw, cb, alog, dtb, dsk, sn, scw, scn)


def _post_kernel(x_ref, oa0_ref, oa1_ref, ybc_ref, woa_ref, wobc_ref, gpost_ref, gpre_ref,
                 gmlp_ref, wup_ref, wdn_ref, o_ref, acc_ref, *, tf):
    oa = jnp.concatenate([oa0_ref[...], oa1_ref[...]], axis=1).astype(BF16)
    mix = jnp.dot(oa, woa_ref[...], preferred_element_type=F32)
    mix = mix + jnp.dot(ybc_ref[...], wobc_ref[...], preferred_element_type=F32)
    x1 = x_ref[...] + _rms(mix, gpost_ref[...])
    h = _rms(x1, gpre_ref[...]).astype(BF16)
    for j in range(D_FF // tf):
        u = jnp.dot(h, wup_ref[:, j * tf:(j + 1) * tf], preferred_element_type=F32)
        u = jnp.square(jnp.maximum(u, 0.0)).astype(BF16)
        f = jnp.dot(u, wdn_ref[j * tf:(j + 1) * tf, :], preferred_element_type=F32)
        if j == 0:
            acc_ref[...] = f
        else:
            acc_ref[...] += f
    o_ref[...] = x1 + _rms(acc_ref[...], gmlp_ref[...])


def _post(x2d, oa0, oa1, ybc, w_out_a, w_out_bc, gpost, gpre, gmlp, w_up, w_dn, layer, tm, tf):
    m = x2d.shape[0]

    def resident(shape):
        return pl.BlockSpec((None,) + shape, lambda i: (layer, 0, 0),
                            pipeline_mode=pl.Buffered(1))

    return pl.pallas_call(
        functools.partial(_post_kernel, tf=tf),
        grid=(m // tm,),
        in_specs=[
            pl.BlockSpec((tm, D_MODEL), lambda i: (i, 0)),
            pl.BlockSpec((tm, 128), lambda i: (i, 0)),
            pl.BlockSpec((tm, 128), lambda i: (i, 0)),
            pl.BlockSpec((tm, D_B + D_C), lambda i: (i, 0)),
            resident((D_A, D_MODEL)), resident((D_B + D_C, D_MODEL)),
            resident((1, D_MODEL)), resident((1, D_MODEL)), resident((1, D_MODEL)),
            resident((D_MODEL, D_FF)), resident((D_FF, D_MODEL)),
        ],
        out_specs=pl.BlockSpec((tm, D_MODEL), lambda i: (i, 0)),
        out_shape=jax.ShapeDtypeStruct((m, D_MODEL), F32),
        scratch_shapes=[pltpu.VMEM((tm, D_MODEL), F32)],
        compiler_params=pltpu.CompilerParams(
            dimension_semantics=("parallel",), vmem_limit_bytes=VMEM_LIMIT),
        name="post",
    )(x2d, oa0, oa1, ybc, w_out_a, w_out_bc, gpost, gpre, gmlp, w_up, w_dn)


def _prep_weights(norm_mix_pre, norm_mix_post, norm_mlp_pre, norm_mlp_post, w_in, w_out,
                  attn_norm, ssm_conv_w, ssm_conv_b, ssm_a_log, ssm_dt_bias, ssm_d, ssm_norm,
                  sconv_w, sconv_norm, w_mlp_up, w_mlp_down):
    depth = w_in.shape[0]
    o = 0
    cols = {}
    for name, width in (("q", D_A), ("k", D_A), ("v", D_A), ("z", D_B), ("xbc", CONV_DIM),
                        ("dt", H_B), ("b", D_C), ("c", D_C), ("u", D_C)):
        cols[name] = (o, o + width)
        o += width

    def cs(name):
        lo, hi = cols[name]
        return w_in[:, :, lo:hi]

    pad_dt = jnp.zeros((depth, D_MODEL, 128 - H_B), w_in.dtype)
    w_in_r = jnp.concatenate(
        [cs("xbc"), cs("z"), cs("q"), cs("k"), cs("v"), cs("b"), cs("c"), cs("u"),
         cs("dt"), pad_dt], axis=2).astype(BF16)

    def lane_pad(v):
        return jnp.pad(v.astype(F32), ((0, 0), (0, 128 - H_B)))[:, None, :]

    def row(v):
        return v.astype(F32)[:, None, :]

    return dict(
        g_mix_pre=row(norm_mix_pre), g_mix_post=row(norm_mix_post),
        g_mlp_pre=row(norm_mlp_pre), g_mlp_post=row(norm_mlp_post),
        w_in=w_in_r,
        w_out_a=w_out[:, 0:D_A, :].astype(BF16),
        w_out_bc=w_out[:, D_A:, :].astype(BF16),
        attn_norm=row(attn_norm),
        ssd=(ssm_conv_w.astype(F32), row(ssm_conv_b), lane_pad(ssm_a_log),
             lane_pad(ssm_dt_bias), row(jnp.repeat(ssm_d, HEAD_DIM, axis=1)),
             row(ssm_norm), sconv_w.astype(F32), row(sconv_norm)),
        w_up=w_mlp_up.astype(BF16), w_dn=w_mlp_down.astype(BF16),
    )


def _run_trunk(x, states, w, prompt):
    batch, seq, _ = x.shape
    depth = w["w_in"].shape[0]
    m = batch * seq
    x2d = x.reshape(m, D_MODEL)
    tm = 512 if m % 512 == 0 else m
    outs = [[] for _ in range(5)]
    if prompt:
        chunk, nc = SSD_CHUNK, seq // SSD_CHUNK
        wb = min(WIN_MAX, seq)
    else:
        chunk, nc = seq, 1
        cache_k, cache_v, st_ssm, st_conv, st_sconv = states
        wb = cache_k.shape[2]
        cache_k = jnp.transpose(cache_k, (0, 1, 3, 4, 2)).reshape(depth, batch, D_A, wb)
        cache_v = jnp.transpose(cache_v, (0, 1, 3, 4, 2)).reshape(depth, batch, D_A, wb)
        new_k = jnp.zeros((depth, batch, D_A, wb), F32)
        new_v = jnp.zeros((depth, batch, D_A, wb), F32)
    for l in range(depth):
        if prompt:
            proj, qkv_perm = _in_proj(x2d, w["g_mix_pre"], w["w_in"], l, tm, seq)
            oa0, oa1 = _attn_prompt(qkv_perm, w["attn_norm"], l, batch, seq)
            p3 = proj.reshape(batch, seq, PROJ_W)
            new_k = p3[:, seq - wb:, COL_Q + D_A:COL_Q + 2 * D_A]
            new_v = p3[:, seq - wb:, COL_Q + 2 * D_A:COL_Q + 3 * D_A]
            h0 = jnp.zeros((batch, H_B, HEAD_DIM, N_STATE), F32)
            cprev8 = jnp.zeros((batch, 8, CONV_DIM), F32)
            sprev8 = jnp.zeros((batch, 8, D_C), F32)
        else:
            (proj,) = _in_proj(x2d, w["g_mix_pre"], w["w_in"], l, tm)
            oa0, oa1, new_k, new_v = _attn_sample(proj, cache_k, cache_v, w["attn_norm"],
                                                  new_k, new_v, l, batch, seq, wb)
            h0 = st_ssm[l].astype(F32)
            cprev8 = jnp.pad(st_conv[l].astype(F32), ((0, 0), (8 - (CONV_W - 1), 0), (0, 0)))
            sprev8 = jnp.pad(st_sconv[l].astype(F32), ((0, 0), (8 - (SCONV_W - 1), 0), (0, 0)))
        ybc, h_new, c8, s8 = _ssd(proj, h0, cprev8, sprev8, w["ssd"], l, batch, nc,
                                  chunk, SSD_GROUP if batch % SSD_GROUP == 0 else 1)
        ybc = ybc.reshape(m, D_B + D_C)
        x2d = _post(x2d, oa0, oa1, ybc, w["w_out_a"], w["w_out_bc"], w["g_mix_post"],
                    w["g_mlp_pre"], w["g_mlp_post"], w["w_up"], w["w_dn"], l, tm, 1024)
        if prompt:
            outs[0].append(new_k.reshape(batch, wb, H_A, HEAD_DIM))
            outs[1].append(new_v.reshape(batch, wb, H_A, HEAD_DIM))
        outs[2].append(h_new)
        outs[3].append(c8[:, 8 - (CONV_W - 1):, :])
        outs[4].append(s8[:, 8 - (SCONV_W - 1):, :])
    if not prompt:
        outs[0] = jnp.transpose(new_k.reshape(depth, batch, H_A, HEAD_DIM, wb), (0, 1, 4, 2, 3))
        outs[1] = jnp.transpose(new_v.reshape(depth, batch, H_A, HEAD_DIM, wb), (0, 1, 4, 2, 3))
    stacked = tuple(o if not isinstance(o, list) else jnp.stack(o) for o in outs)
    return x2d.reshape(batch, seq, D_MODEL), stacked


def kernel(x_prompt, x_sample, cache_attn_k, cache_attn_v, state_ssm, state_ssm_conv, state_sconv, norm_mix_pre, norm_mix_post, norm_mlp_pre, norm_mlp_post, w_in, w_out, attn_norm, ssm_conv_w, ssm_conv_b, ssm_a_log, ssm_dt_bias, ssm_d, ssm_norm, sconv_w, sconv_norm, w_mlp_up, w_mlp_down):
    w = _prep_weights(norm_mix_pre, norm_mix_post, norm_mlp_pre, norm_mlp_post, w_in, w_out,
                      attn_norm, ssm_conv_w, ssm_conv_b, ssm_a_log, ssm_dt_bias, ssm_d,
                      ssm_norm, sconv_w, sconv_norm, w_mlp_up, w_mlp_down)
    y_prompt, (p_k, p_v, p_ssm, p_conv, p_sconv) = _run_trunk(x_prompt, None, w, True)
    states = (cache_attn_k, cache_attn_v, state_ssm, state_ssm_conv, state_sconv)
    y_sample, (s_k, s_v, s_ssm, s_conv, s_sconv) = _run_trunk(x_sample, states, w, False)
    return (y_prompt, y_sample, p_k, p_v, p_ssm, p_conv, p_sconv,
            s_k, s_v, s_ssm, s_conv, s_sconv)
```

```python
import functools
import math

import jax
import jax.numpy as jnp
from jax import lax
from jax.experimental import pallas as pl
from jax.experimental.pallas import tpu as pltpu

F32 = jnp.float32
BF16 = jnp.bfloat16

HEAD_DIM = 64
D_MODEL = 1024
D_A = 256
H_A = 4
D_B = 512
H_B = 8
SSM_GROUPS = 2
N_STATE = 128
CONV_W = 4
CONV_DIM = D_B + 2 * SSM_GROUPS * N_STATE
D_C = 256
SCONV_W = 3
D_FF = 4 * D_MODEL
DILATED_CONFIGS = ((128, 1), (512, 4), (2048, 16))
WIN_MAX = 2048
EPS = 1e-6
LOG2E = 1.4426950408889634
ALIBI_SLOPES = tuple(2.0 ** (-8.0 * (h + 1) / H_A) for h in range(H_A))
ATTN_BLOCK = 128
ATTN_PERM = 16
ATTN_BLOCKS_PER_STEP = 16
SSD_CHUNK = 128
SSD_GROUP = 2
SSD_STAGES = 6

COL_XBC = 0
COL_Z = 1024
COL_Q = 1536
COL_G = 2304
COL_DT = 3072
PROJ_W = 3200

VMEM_LIMIT = 56 * 1024 * 1024


def _rms(x, g):
    return x * lax.rsqrt(jnp.mean(x * x, axis=-1, keepdims=True) + EPS) * g


def _silu(x):
    return x * (1.0 / (1.0 + jnp.exp(-x)))


def _softplus(x):
    return jnp.maximum(x, 0.0) + jnp.log1p(jnp.exp(-jnp.abs(x)))


def _in_proj_kernel(x_ref, g_ref, w_ref, o_ref, *rest, permute):
    h = _rms(x_ref[...], g_ref[...]).astype(BF16)
    for lo, hi in ((0, COL_Q), (COL_Q, COL_G), (COL_G, PROJ_W)):
        res = jnp.dot(h, w_ref[:, lo:hi], preferred_element_type=F32)
        o_ref[:, lo:hi] = res
        if permute and lo == COL_Q:
            operm_ref, scr_ref = rest
            tm = x_ref.shape[0]
            for j in range(3 * D_A // 128):
                scr_ref[j] = res[:, j * 128:(j + 1) * 128]
            for r in range(ATTN_PERM):
                for j in range(3 * D_A // 128):
                    operm_ref[r, :, j * 128:(j + 1) * 128] = (
                        scr_ref.at[j][pl.ds(r, tm // ATTN_PERM, stride=ATTN_PERM), :])


def _in_proj(x2d, g, w, layer, tm, seq=None):
    m = x2d.shape[0]
    permute = seq is not None
    out_specs = [pl.BlockSpec((tm, PROJ_W), lambda i: (i, 0))]
    out_shape = [jax.ShapeDtypeStruct((m, PROJ_W), F32)]
    scratch = []
    if permute:
        tiles = seq // tm
        rows = tm // ATTN_PERM
        out_specs.append(pl.BlockSpec((None, ATTN_PERM, rows, 3 * D_A),
                                      lambda i: (i // tiles, 0, i % tiles, 0)))
        out_shape.append(jax.ShapeDtypeStruct((m // seq, ATTN_PERM, seq // ATTN_PERM, 3 * D_A), F32))
        scratch.append(pltpu.VMEM((3 * D_A // 128, tm, 128), F32))
    return pl.pallas_call(
        functools.partial(_in_proj_kernel, permute=permute),
        grid=(m // tm,),
        in_specs=[
            pl.BlockSpec((tm, D_MODEL), lambda i: (i, 0)),
            pl.BlockSpec((None, 1, D_MODEL), lambda i: (layer, 0, 0)),
            pl.BlockSpec((None, D_MODEL, PROJ_W), lambda i: (layer, 0, 0),
                         pipeline_mode=pl.Buffered(1)),
        ],
        out_specs=out_specs,
        out_shape=out_shape,
        scratch_shapes=scratch,
        compiler_params=pltpu.CompilerParams(
            dimension_semantics=("parallel",), vmem_limit_bytes=VMEM_LIMIT),
        name="in_proj",
    )(x2d, g, w)


def _attn_prompt_kernel(q_ref, k_ref, v_ref, g_ref, o0_ref, o1_ref,
                        acc_ref, m_ref, l_ref, bias_ref, *, seq):
    blk = ATTN_BLOCK
    nbr = len(DILATED_CONFIGS)
    P = ATTN_PERM
    per_res = seq // P

    def local_to_strided(a, dil):
        nchunk = P // dil
        rows = blk // nchunk
        return nchunk * (a % rows) + a // rows

    shape4 = (H_A * blk, 2 * blk)
    row4 = lax.broadcasted_iota(jnp.int32, shape4, 0)
    ki = lax.broadcasted_iota(jnp.int32, shape4, 1)
    slope = jnp.zeros(shape4, F32)
    for h in range(H_A):
        slope = jnp.where(row4 // blk == h, ALIBI_SLOPES[h], slope)
    for bi, (win, dil) in enumerate(DILATED_CONFIGS):
        assert win // dil == blk and P % dil == 0
        jq = blk + local_to_strided(row4 % blk, dil)
        jk = local_to_strided(ki % blk, dil) + blk * (ki // blk)
        diff = jq - jk
        band = (diff >= 0) & (diff <= blk)
        dist = slope * (-float(dil) * LOG2E * diff.astype(F32))
        bias_ref[bi] = jnp.where(band, dist, -jnp.inf)
        bias_ref[nbr + bi] = jnp.where(band & (ki >= blk), dist, -jnp.inf)

    lane_head = lax.broadcasted_iota(jnp.int32, (blk, D_A), 1) // HEAD_DIM
    lane_lo = lax.broadcasted_iota(jnp.int32, (blk, 128), 1) < HEAD_DIM

    def per_head(col):
        c = [col[h * blk:(h + 1) * blk, :] for h in range(H_A)]
        return jnp.concatenate([jnp.where(lane_lo, c[0], c[1]), jnp.where(lane_lo, c[2], c[3])],
                               axis=1)

    def blocks(r, u, bi, dil):
        nchunk = P // dil
        rows = blk // nchunk
        nb = min(ATTN_BLOCKS_PER_STEP, per_res // rows)
        starts = [pl.multiple_of(jnp.maximum(nb * u - 1 + j, 0) * rows, rows)
                  for j in range(nb + 1)]

        def load(ref, st):
            return jnp.concatenate(
                [ref[dil * c + r, pl.ds(st, rows), :] for c in range(nchunk)], axis=0)

        def store(ref, st, val):
            for c in range(nchunk):
                ref[dil * c + r, pl.ds(st, rows), :] = val[c * rows:(c + 1) * rows, :]

        kb = [load(k_ref, st).astype(BF16) for st in starts]
        vb = [load(v_ref, st).astype(BF16) for st in starts]
        results = []
        for j in range(nb):
            q = (load(q_ref, starts[1 + j]) * (HEAD_DIM ** -0.5 * LOG2E)).astype(BF16)
            q4 = jnp.concatenate(
                [jnp.where(lane_head == h, q, jnp.zeros_like(q)) for h in range(H_A)], axis=0)
            kk = jnp.concatenate([kb[j], kb[j + 1]], axis=0)
            vv = jnp.concatenate([vb[j], vb[j + 1]], axis=0)
            s = lax.dot_general(q4, kk, (((1,), (1,)), ((), ())), preferred_element_type=F32)
            if j == 0:
                s = s + bias_ref[jnp.where(u == 0, nbr + bi, bi)]
            else:
                s = s + bias_ref[bi]
            m4 = jnp.max(s, axis=-1, keepdims=True)
            p = jnp.exp2(s - m4)
            l4 = jnp.sum(p, axis=-1, keepdims=True)
            pb = p.astype(BF16)
            o = [jnp.dot(pb[h * blk:(h + 1) * blk, :],
                         vv[:, (h // 2) * 128:(h // 2 + 1) * 128],
                         preferred_element_type=F32) for h in range(H_A)]
            acc_b = jnp.concatenate([jnp.where(lane_lo, o[0], o[1]),
                                     jnp.where(lane_lo, o[2], o[3])], axis=1)
            m_b = per_head(m4)
            l_b = per_head(l4)
            if bi > 0:
                st = starts[1 + j]
                m_old = load(m_ref, st)
                m_new = jnp.maximum(m_old, m_b)
                a_old = jnp.exp2(m_old - m_new)
                a_b = jnp.exp2(m_b - m_new)
                acc_b = load(acc_ref, st) * a_old + acc_b * a_b
                l_b = load(l_ref, st) * a_old + l_b * a_b
                m_b = m_new
            results.append((acc_b, m_b, l_b))
        for j in range(nb):
            store(acc_ref, starts[1 + j], results[j][0])
            store(m_ref, starts[1 + j], results[j][1])
            store(l_ref, starts[1 + j], results[j][2])

    for bi, (win, dil) in enumerate(DILATED_CONFIGS):
        nblk = per_res // (blk // (P // dil))
        nb = min(ATTN_BLOCKS_PER_STEP, nblk)
        nstep = nblk // nb
        nres = ATTN_BLOCKS_PER_STEP // nb
        assert dil % nres == 0

        def body(it, carry, bi=bi, dil=dil, nstep=nstep, nres=nres):
            rg = it // nstep
            u = it - rg * nstep
            for rr in range(nres):
                blocks(rg * nres + rr, u, bi, dil)
            return carry

        lax.fori_loop(0, (dil // nres) * nstep, body, 0)

    def finish(r, carry):
        o = _rms(acc_ref[r] / l_ref[r], g_ref[...])
        o0_ref[pl.ds(r, per_res, stride=P), :] = o[:, 0:128]
        o1_ref[pl.ds(r, per_res, stride=P), :] = o[:, 128:256]
        return carry

    lax.fori_loop(0, P, finish, 0)


def _attn_prompt(qkv_perm, g, layer, batch, seq):
    per_res = seq // ATTN_PERM
    qkv_specs = [pl.BlockSpec((None, ATTN_PERM, per_res, D_A),
                              functools.partial(lambda b, j: (b, 0, 0, j), j=j)) for j in range(3)]
    half_spec = pl.BlockSpec((seq, 128), lambda b: (b, 0))
    return pl.pallas_call(
        functools.partial(_attn_prompt_kernel, seq=seq),
        grid=(batch,),
        in_specs=qkv_specs + [pl.BlockSpec((None, 1, D_A), lambda b: (layer, 0, 0))],
        out_specs=[half_spec, half_spec],
        out_shape=[jax.ShapeDtypeStruct((batch * seq, 128), F32)] * 2,
        scratch_shapes=[pltpu.VMEM((ATTN_PERM, per_res, D_A), F32)] * 3
        + [pltpu.VMEM((2 * len(DILATED_CONFIGS), H_A * ATTN_BLOCK, 2 * ATTN_BLOCK), F32)],
        compiler_params=pltpu.CompilerParams(
            dimension_semantics=("parallel",), vmem_limit_bytes=VMEM_LIMIT),
        name="attn_prompt",
    )(qkv_perm, qkv_perm, qkv_perm, g)


def _attn_sample_kernel(q_ref, k_ref, v_ref, ckt_ref, cvt_ref, g_ref, nk_in_ref, nv_in_ref,
                        o0_ref, o1_ref, nkt_ref, nvt_ref, kpad_ref, vpad_ref, *, wb, s_len):
    pad = 128
    k_new = k_ref[...]
    v_new = v_ref[...]
    kpad_ref[...] = jnp.zeros(kpad_ref.shape, F32)
    vpad_ref[...] = jnp.zeros(vpad_ref.shape, F32)
    kpad_ref[pad - s_len:pad, :] = k_new
    vpad_ref[pad - s_len:pad, :] = v_new
    kpad = kpad_ref[...]
    vpad = vpad_ref[...]
    kpad_t = kpad.T
    vpad_t = vpad.T

    tail = lax.broadcasted_iota(jnp.int32, (D_A, pad), 1) >= pad - s_len
    for src_ref, dst_ref, new_t in ((ckt_ref, nkt_ref, kpad_t), (cvt_ref, nvt_ref, vpad_t)):
        rolled = pltpu.roll(src_ref[...], wb - s_len, axis=1)
        dst_ref[:, 0:wb - pad] = rolled[:, 0:wb - pad]
        dst_ref[:, wb - pad:wb] = jnp.where(tail, new_t, rolled[:, wb - pad:wb])

    rows = H_A * s_len
    ncol = wb + pad
    row_i = lax.broadcasted_iota(jnp.int32, (rows, ncol), 0)
    col_i = lax.broadcasted_iota(jnp.int32, (rows, ncol), 1)
    s_i = row_i % s_len
    pos = jnp.where(col_i < wb, col_i, col_i + s_len - pad)
    d = wb + s_i - pos
    real = (col_i < wb) | (col_i >= ncol - s_len)
    mult = jnp.zeros((rows, ncol), F32)
    for win, dil in DILATED_CONFIGS:
        hit = real & (d >= 0) & (d <= win) & ((d % dil) == 0)
        mult = mult + jnp.where(hit, 1.0, 0.0)
    slope = jnp.zeros((rows, ncol), F32)
    for h in range(H_A):
        slope = jnp.where(row_i // s_len == h, ALIBI_SLOPES[h], slope)
    bias = slope * d.astype(F32)

    q = q_ref[...] * (HEAD_DIM ** -0.5)
    lane_head = lax.broadcasted_iota(jnp.int32, (s_len, D_A), 1) // HEAD_DIM
    qh = jnp.concatenate([jnp.where(lane_head == h, q, 0.0) for h in range(H_A)],
                         axis=0).astype(BF16)
    s1 = jnp.dot(qh, ckt_ref[...].astype(BF16), preferred_element_type=F32)
    s2 = jnp.dot(qh, kpad_t.astype(BF16), preferred_element_type=F32)
    s = jnp.concatenate([s1, s2], axis=1) - bias
    s = jnp.where(mult > 0.0, s, -jnp.inf)
    m = jnp.max(s, axis=-1, keepdims=True)
    p = jnp.exp(s - m) * mult
    l = jnp.sum(p, axis=-1, keepdims=True)
    pb = p.astype(BF16)
    nt = (((1,), (1,)), ((), ()))
    o = lax.dot_general(pb[:, 0:wb], cvt_ref[...].astype(BF16), nt, preferred_element_type=F32)
    o = o + jnp.dot(pb[:, wb:ncol], vpad.astype(BF16), preferred_element_type=F32)
    o = o / l
    out = jnp.zeros((s_len, D_A), F32)
    for h in range(H_A):
        out = jnp.where(lane_head == h, o[h * s_len:(h + 1) * s_len, :], out)
    out = _rms(out, g_ref[...])
    o0_ref[...] = out[:, 0:128]
    o1_ref[...] = out[:, 128:256]


def _attn_sample(proj, cache_k, cache_v, g, new_k, new_v, layer, batch, s_len, wb):
    depth = cache_k.shape[0]
    qb = COL_Q // D_A
    cache_spec = pl.BlockSpec((None, None, D_A, wb), lambda b: (layer, b, 0, 0))
    return pl.pallas_call(
        functools.partial(_attn_sample_kernel, wb=wb, s_len=s_len),
        grid=(batch,),
        in_specs=[
            pl.BlockSpec((s_len, D_A), lambda b: (b, qb)),
            pl.BlockSpec((s_len, D_A), lambda b: (b, qb + 1)),
            pl.BlockSpec((s_len, D_A), lambda b: (b, qb + 2)),
            cache_spec, cache_spec,
            pl.BlockSpec((None, 1, D_A), lambda b: (layer, 0, 0)),
            pl.BlockSpec(memory_space=pl.ANY), pl.BlockSpec(memory_space=pl.ANY),
        ],
        out_specs=[pl.BlockSpec((s_len, 128), lambda b: (b, 0))] * 2 + [cache_spec, cache_spec],
        out_shape=[jax.ShapeDtypeStruct((batch * s_len, 128), F32)] * 2 + [
                   jax.ShapeDtypeStruct((depth, batch, D_A, wb), F32)] * 2,
        input_output_aliases={6: 2, 7: 3},
        scratch_shapes=[pltpu.VMEM((128, D_A), F32)] * 2,
        compiler_params=pltpu.CompilerParams(
            dimension_semantics=("parallel",), vmem_limit_bytes=VMEM_LIMIT),
        name="attn_sample",
    )(proj, proj, proj, cache_k, cache_v, g, new_k, new_v)


def _ssd_kernel(xbc_ref, z_ref, gate_ref, dtr_ref, h0_ref, cprev_ref, sprev_ref,
                cw_ref, cb_ref, alog_ref, dtb_ref, dsk_ref, sn_ref, scw_ref, scn_ref,
                y_ref, hout_ref, cnew_ref, snew_ref,
                ext_ref, ext2_ref, h_ref, *, chunk, group):
    c = pl.program_id(1)
    nc = pl.num_programs(1)

    @pl.when(c == 0)
    def _init():
        h_ref[...] = h0_ref[...]
        ext_ref[:, 0:8, :] = cprev_ref[...]
        ext2_ref[:, 0:8, :] = sprev_ref[...]

    for i in range(group):
        _ssd_one(xbc_ref.at[i], z_ref.at[i], gate_ref.at[i], dtr_ref.at[i],
                 cw_ref, cb_ref, alog_ref, dtb_ref, dsk_ref, sn_ref, scw_ref, scn_ref,
                 y_ref.at[i], ext_ref.at[i], ext2_ref.at[i], h_ref.at[i], chunk=chunk)

    @pl.when(c == nc - 1)
    def _final():
        hout_ref[...] = h_ref[...]
        cnew_ref[...] = ext_ref[:, chunk:chunk + 8, :]
        snew_ref[...] = ext2_ref[:, chunk:chunk + 8, :]

    ext_ref[:, 0:8, :] = ext_ref[:, chunk:chunk + 8, :]
    ext2_ref[:, 0:8, :] = ext2_ref[:, chunk:chunk + 8, :]


def _ssd_one(*args, **kwargs):
    for _ in _ssd_stages(*args, **kwargs):
        pass


def _ssd_stages(xbc_ref, z_ref, gate_ref, dtr_ref,
                cw_ref, cb_ref, alog_ref, dtb_ref, dsk_ref, sn_ref, scw_ref, scn_ref,
                y_ref, ext_ref, ext2_ref, h_ref, *, chunk):
    C = chunk
    heads_per_group = H_B // SSM_GROUPS
    gw = heads_per_group * HEAD_DIM

    ext_ref[8:8 + C, :] = xbc_ref[...]
    xc_parts = []
    half = CONV_DIM // 2
    for lo in (0, half):
        conv = xbc_ref[:, lo:lo + half] * cw_ref[CONV_W - 1:CONV_W, lo:lo + half]
        for k in range(CONV_W - 1):
            off = 8 - (CONV_W - 1) + k
            conv = conv + ext_ref[off:off + C, lo:lo + half] * cw_ref[k:k + 1, lo:lo + half]
        xc_parts.append(_silu(conv + cb_ref[:, lo:lo + half]))
        yield
    x_s = xc_parts[0]
    bm = xc_parts[1][:, 0:SSM_GROUPS * N_STATE]
    cm = xc_parts[1][:, SSM_GROUPS * N_STATE:2 * SSM_GROUPS * N_STATE]

    gates = gate_ref[...]
    b_gate = gates[:, 0:D_C]
    prod = gates[:, D_C:2 * D_C] * gates[:, 2 * D_C:3 * D_C]
    ext2_ref[8:8 + C, :] = prod
    sconv = prod * scw_ref[SCONV_W - 1:SCONV_W, :]
    for k in range(SCONV_W - 1):
        off = 8 - (SCONV_W - 1) + k
        sconv = sconv + ext2_ref[off:off + C, :] * scw_ref[k:k + 1, :]
    y_ref[:, D_B:D_B + D_C] = _rms(b_gate * sconv, scn_ref[...]).astype(y_ref.dtype)

    row = lax.broadcasted_iota(jnp.int32, (C, C), 0)
    col = lax.broadcasted_iota(jnp.int32, (C, C), 1)
    causal = row >= col
    dt8 = _softplus((dtr_ref[...] + dtb_ref[...]).T[0:H_B, :])
    a8 = jnp.broadcast_to(-jnp.exp(alog_ref[...]), (128, 128)).T[0:H_B, 0:C]
    da = dt8 * a8
    upper = jnp.where(row <= col, 1.0, 0.0).astype(BF16)
    da_hi = da.astype(BF16)
    rem = da - da_hi.astype(F32)
    da_mid = rem.astype(BF16)
    da_lo = (rem - da_mid.astype(F32)).astype(BF16)
    acs8 = (jnp.dot(da_hi, upper, preferred_element_type=F32)
            + jnp.dot(da_mid, upper, preferred_element_type=F32)
            + jnp.dot(da_lo, upper, preferred_element_type=F32))
    last8 = acs8[:, C - 1:C]
    cdec8 = jnp.exp(last8)
    stacked = jnp.concatenate(
        [acs8, jnp.exp(acs8), jnp.exp(last8 - acs8) * dt8, jnp.zeros((128 - 3 * H_B, C), F32)],
        axis=0)
    cols = stacked.T
    lane_lo = lax.broadcasted_iota(jnp.int32, (C, 2 * HEAD_DIM), 1) < HEAD_DIM

    def expand(base):
        return jnp.concatenate(
            [jnp.where(lane_lo, cols[:, base + 2 * p:base + 2 * p + 1],
                       cols[:, base + 2 * p + 1:base + 2 * p + 2]) for p in range(H_B // 2)],
            axis=1)

    e_acs_x = expand(H_B)
    w_end_x = expand(2 * H_B)
    yield

    nt = (((1,), (1,)), ((), ()))
    tn = (((0,), (0,)), ((), ()))
    x_bf = x_s.astype(BF16)
    xw = (x_s * w_end_x)
    y_parts = []
    for g in range(SSM_GROUPS):
        bm_g = bm[:, g * N_STATE:(g + 1) * N_STATE].astype(BF16)
        cm_g = cm[:, g * N_STATE:(g + 1) * N_STATE].astype(BF16)
        scores = lax.dot_general(cm_g, bm_g, nt, preferred_element_type=F32)
        h_g = h_ref[g * heads_per_group:(g + 1) * heads_per_group].reshape(gw, N_STATE)
        y_off = lax.dot_general(cm_g, h_g.astype(BF16), nt, preferred_element_type=F32)
        y_off = y_off * e_acs_x[:, g * gw:(g + 1) * gw]
        diag = []
        for pair in range(heads_per_group // 2):
            outs = []
            for hh in range(2):
                h = g * heads_per_group + 2 * pair + hh
                seg = cols[:, h:h + 1] - acs8[h:h + 1, :]
                decay = jnp.exp(jnp.where(causal, seg, -jnp.inf))
                mh = (scores * decay * dt8[h:h + 1, :]).astype(BF16)
                lo = (g * heads_per_group + 2 * pair) * HEAD_DIM
                outs.append(jnp.dot(mh, x_bf[:, lo:lo + 2 * HEAD_DIM],
                                    preferred_element_type=F32))
            diag.append(jnp.where(lane_lo, outs[0], outs[1]))
        y_parts.append(jnp.concatenate(diag, axis=1) + y_off)
        st = lax.dot_general(xw[:, g * gw:(g + 1) * gw].astype(BF16), bm_g, tn,
                             preferred_element_type=F32)
        for hh in range(heads_per_group):
            h = g * heads_per_group + hh
            h_ref[h] = h_ref[h] * cdec8[h:h + 1, 0:1] + st[hh * HEAD_DIM:(hh + 1) * HEAD_DIM, :]
        yield
    y = jnp.concatenate(y_parts, axis=1) + dsk_ref[...] * x_s
    y_ref[:, 0:D_B] = _rms(y * _silu(z_ref[...]), sn_ref[...]).astype(y_ref.dtype)
    yield


def _ssd(proj, h0, cprev8, sprev8, wts, layer, batch, nc, chunk, group):
    (cw, cb, alog, dtb, dsk, sn, scw, scn) = wts
    proj3 = proj.reshape(batch, nc * chunk, PROJ_W)

    def tok(width, blk_idx):
        return pl.BlockSpec((group, chunk, width), lambda b, c: (b, c, blk_idx))

    def per_layer(shape):
        return pl.BlockSpec((None,) + shape, lambda b, c: (layer,) + (0,) * len(shape))

    state_spec = pl.BlockSpec((group, H_B, HEAD_DIM, N_STATE), lambda b, c: (b, 0, 0, 0))
    conv_spec = pl.BlockSpec((group, 8, CONV_DIM), lambda b, c: (b, 0, 0))
    sconv_spec = pl.BlockSpec((group, 8, D_C), lambda b, c: (b, 0, 0))

    def stacked(spec):
        return pl.BlockSpec((None,) + tuple(spec.block_shape),
                            lambda b, c: (layer,) + tuple(spec.index_map(b, c)))

    return pl.pallas_call(
        functools.partial(_ssd_kernel, chunk=chunk, group=group),
        grid=(batch // group, nc),
        in_specs=[
            tok(CONV_DIM, COL_XBC // CONV_DIM),
            tok(D_B, COL_Z // D_B),
            tok(3 * D_C, COL_G // (3 * D_C)),
            tok(128, COL_DT // 128),
            stacked(state_spec), stacked(conv_spec), stacked(sconv_spec),
            per_layer((CONV_W, CONV_DIM)), per_layer((1, CONV_DIM)),
            per_layer((1, 128)), per_layer((1, 128)), per_layer((1, D_B)),
            per_layer((1, D_B)), per_layer((SCONV_W, D_C)), per_layer((1, D_C)),
        ],
        out_specs=[
            pl.BlockSpec((group, chunk, D_B + D_C), lambda b, c: (b, c, 0)),
            state_spec, conv_spec, sconv_spec,
        ],
        out_shape=[
            jax.ShapeDtypeStruct((batch, nc * chunk, D_B + D_C), BF16),
            jax.ShapeDtypeStruct((batch, H_B, HEAD_DIM, N_STATE), F32),
            jax.ShapeDtypeStruct((batch, 8, CONV_DIM), F32),
            jax.ShapeDtypeStruct((batch, 8, D_C), F32),
        ],
        scratch_shapes=[
            pltpu.VMEM((group, chunk + 8, CONV_DIM), F32),
            pltpu.VMEM((group, chunk + 8, D_C), F32),
            pltpu.VMEM((group, H_B, HEAD_DIM, N_STATE), F32),
        ],
        compiler_params=pltpu.CompilerParams(
            dimension_semantics=("parallel", "arbitrary"), vmem_limit_bytes=VMEM_LIMIT),
        name="ssd_sconv",
    )(proj3, proj3, proj3, proj3, h0, cprev8, sprev8, cw, cb, alog, dtb, dsk, sn, scw, scn)


def _post_kernel(x_ref, oa0_ref, oa1_ref, ybc_ref, woa_ref, wobc_ref, gpost_ref, gpre_ref,
                 gmlp_ref, wup_ref, wdn_ref, o_ref, acc_ref, *, tf):
    oa = jnp.concatenate([oa0_ref[...], oa1_ref[...]], axis=1).astype(BF16)
    mix = jnp.dot(oa, woa_ref[...], preferred_element_type=F32)
    mix = mix + jnp.dot(ybc_ref[...], wobc_ref[...], preferred_element_type=F32)
    x1 = x_ref[...] + _rms(mix, gpost_ref[...])
    h = _rms(x1, gpre_ref[...]).astype(BF16)
    for j in range(D_FF // tf):
        u = jnp.dot(h, wup_ref[:, j * tf:(j + 1) * tf], preferred_element_type=F32)
        u = jnp.square(jnp.maximum(u, 0.0)).astype(BF16)
        f = jnp.dot(u, wdn_ref[j * tf:(j + 1) * tf, :], preferred_element_type=F32)
        if j == 0:
            acc_ref[...] = f
        else:
            acc_ref[...] += f
    o_ref[...] = x1 + _rms(acc_ref[...], gmlp_ref[...])


def _post(x2d, oa0, oa1, ybc, w_out_a, w_out_bc, gpost, gpre, gmlp, w_up, w_dn, layer, tm, tf):
    m = x2d.shape[0]

    def resident(shape):
        return pl.BlockSpec((None,) + shape, lambda i: (layer, 0, 0),
                            pipeline_mode=pl.Buffered(1))

    return pl.pallas_call(
        functools.partial(_post_kernel, tf=tf),
        grid=(m // tm,),
        in_specs=[
            pl.BlockSpec((tm, D_MODEL), lambda i: (i, 0)),
            pl.BlockSpec((tm, 128), lambda i: (i, 0)),
            pl.BlockSpec((tm, 128), lambda i: (i, 0)),
            pl.BlockSpec((tm, D_B + D_C), lambda i: (i, 0)),
            resident((D_A, D_MODEL)), resident((D_B + D_C, D_MODEL)),
            resident((1, D_MODEL)), resident((1, D_MODEL)), resident((1, D_MODEL)),
            resident((D_MODEL, D_FF)), resident((D_FF, D_MODEL)),
        ],
        out_specs=pl.BlockSpec((tm, D_MODEL), lambda i: (i, 0)),
        out_shape=jax.ShapeDtypeStruct((m, D_MODEL), F32),
        scratch_shapes=[pltpu.VMEM((tm, D_MODEL), F32)],
        compiler_params=pltpu.CompilerParams(
            dimension_semantics=("parallel",), vmem_limit_bytes=VMEM_LIMIT),
        name="post",
    )(x2d, oa0, oa1, ybc, w_out_a, w_out_bc, gpost, gpre, gmlp, w_up, w_dn)


def _fused_kernel(x_ref, oa0_ref, oa1_ref, woa_ref, wobc_ref, gpost_ref, gpre_ref, gmlp_ref,
                  wup_ref, wdn_ref,
                  xbc_ref, z_ref, gate_ref, dtr_ref,
                  cw_ref, cb_ref, alog_ref, dtb_ref, dsk_ref, sn_ref, scw_ref, scn_ref,
                  o_ref, hout_ref, cnew_ref, snew_ref,
                  acc_ref, ybc_ref, ext_ref, ext2_ref, h_ref, *, tf, tm, tiles_per_seq, ntiles):
    i = pl.program_id(0)
    slot = i % 2
    nchunks = tm // SSD_CHUNK

    @pl.when(i == 0)
    def _first():
        ybc_ref[...] = jnp.zeros(ybc_ref.shape, ybc_ref.dtype)

    @pl.when(i % tiles_per_seq == 0)
    def _new_sequence():
        h_ref[...] = jnp.zeros(h_ref.shape, F32)
        ext_ref[0:8, :] = jnp.zeros((8, CONV_DIM), F32)
        ext2_ref[0:8, :] = jnp.zeros((8, D_C), F32)

    def ssd_chunk(c):
        rows = slice(c * SSD_CHUNK, (c + 1) * SSD_CHUNK)
        yield from _ssd_stages(
            xbc_ref.at[rows], z_ref.at[rows], gate_ref.at[rows], dtr_ref.at[rows],
            cw_ref, cb_ref, alog_ref, dtb_ref, dsk_ref, sn_ref, scw_ref, scn_ref,
            ybc_ref.at[slot, rows], ext_ref, ext2_ref, h_ref, chunk=SSD_CHUNK)
        ext_ref[0:8, :] = ext_ref[SSD_CHUNK:SSD_CHUNK + 8, :]
        ext2_ref[0:8, :] = ext2_ref[SSD_CHUNK:SSD_CHUNK + 8, :]

    def ssd_tile():
        for c in range(nchunks):
            yield from ssd_chunk(c)

    stages = ssd_tile()
    nff = D_FF // tf
    per_ff = -(-(nchunks * SSD_STAGES) // (nff + 1))

    def advance():
        for _ in range(per_ff):
            next(stages, None)

    oa = jnp.concatenate([oa0_ref[...], oa1_ref[...]], axis=1).astype(BF16)
    mix = jnp.dot(oa, woa_ref[...], preferred_element_type=F32)
    mix = mix + jnp.dot(ybc_ref[1 - slot], wobc_ref[...], preferred_element_type=F32)
    x1 = x_ref[...] + _rms(mix, gpost_ref[...])
    h = _rms(x1, gpre_ref[...]).astype(BF16)
    advance()
    for j in range(nff):
        u = jnp.dot(h, wup_ref[:, j * tf:(j + 1) * tf], preferred_element_type=F32)
        u = jnp.square(jnp.maximum(u, 0.0)).astype(BF16)
        f = jnp.dot(u, wdn_ref[j * tf:(j + 1) * tf, :], preferred_element_type=F32)
        if j == 0:
            acc_ref[...] = f
        else:
            acc_ref[...] += f
        advance()
    for _ in stages:
        pass
    o_ref[...] = x1 + _rms(acc_ref[...], gmlp_ref[...])

    @pl.when((i % tiles_per_seq == tiles_per_seq - 1) & (i < ntiles))
    def _sequence_done():
        hout_ref[...] = h_ref[...]
        cnew_ref[...] = ext_ref[0:8, :]
        snew_ref[...] = ext2_ref[0:8, :]


def _fused(x2d, oa0, oa1, proj, w_out_a, w_out_bc, gpost, gpre, gmlp, w_up, w_dn, wts, layer,
           batch, tm, tf):
    m = x2d.shape[0]
    (cw, cb, alog, dtb, dsk, sn, scw, scn) = wts
    ntiles = m // tm
    tps = ntiles // batch

    def resident(shape):
        return pl.BlockSpec((None,) + shape, lambda i: (layer,) + (0,) * len(shape),
                            pipeline_mode=pl.Buffered(1))

    def cur(width, blk_idx):
        return pl.BlockSpec((tm, width), lambda i: (jnp.minimum(i, ntiles - 1), blk_idx))

    def prev(width):
        return pl.BlockSpec((tm, width), lambda i: (jnp.maximum(i - 1, 0), 0))

    def per_seq(shape):
        return pl.BlockSpec((None,) + shape,
                            lambda i: (jnp.minimum(i, ntiles - 1) // tps,) + (0,) * len(shape))

    return pl.pallas_call(
        functools.partial(_fused_kernel, tf=tf, tm=tm, tiles_per_seq=tps, ntiles=ntiles),
        grid=(ntiles + 1,),
        in_specs=[
            prev(D_MODEL), prev(128), prev(128),
            resident((D_A, D_MODEL)), resident((D_B + D_C, D_MODEL)),
            resident((1, D_MODEL)), resident((1, D_MODEL)), resident((1, D_MODEL)),
            resident((D_MODEL, D_FF)), resident((D_FF, D_MODEL)),
            cur(CONV_DIM, COL_XBC // CONV_DIM), cur(D_B, COL_Z // D_B),
            cur(3 * D_C, COL_G // (3 * D_C)), cur(128, COL_DT // 128),
            resident((CONV_W, CONV_DIM)), resident((1, CONV_DIM)),
            resident((1, 128)), resident((1, 128)), resident((1, D_B)),
            resident((1, D_B)), resident((SCONV_W, D_C)), resident((1, D_C)),
        ],
        out_specs=[prev(D_MODEL), per_seq((H_B, HEAD_DIM, N_STATE)), per_seq((8, CONV_DIM)),
                   per_seq((8, D_C))],
        out_shape=[jax.ShapeDtypeStruct((m, D_MODEL), F32),
                   jax.ShapeDtypeStruct((batch, H_B, HEAD_DIM, N_STATE), F32),
                   jax.ShapeDtypeStruct((batch, 8, CONV_DIM), F32),
                   jax.ShapeDtypeStruct((batch, 8, D_C), F32)],
        scratch_shapes=[pltpu.VMEM((tm, D_MODEL), F32),
                        pltpu.VMEM((2, tm, D_B + D_C), BF16),
                        pltpu.VMEM((SSD_CHUNK + 8, CONV_DIM), F32),
                        pltpu.VMEM((SSD_CHUNK + 8, D_C), F32),
                        pltpu.VMEM((H_B, HEAD_DIM, N_STATE), F32)],
        compiler_params=pltpu.CompilerParams(
            dimension_semantics=("arbitrary",), vmem_limit_bytes=VMEM_LIMIT),
        name="ssd_post",
    )(x2d, oa0, oa1, w_out_a, w_out_bc, gpost, gpre, gmlp, w_up, w_dn,
      proj, proj, proj, proj, cw, cb, alog, dtb, dsk, sn, scw, scn)


def _prep_weights(norm_mix_pre, norm_mix_post, norm_mlp_pre, norm_mlp_post, w_in, w_out,
                  attn_norm, ssm_conv_w, ssm_conv_b, ssm_a_log, ssm_dt_bias, ssm_d, ssm_norm,
                  sconv_w, sconv_norm, w_mlp_up, w_mlp_down):
    depth = w_in.shape[0]
    o = 0
    cols = {}
    for name, width in (("q", D_A), ("k", D_A), ("v", D_A), ("z", D_B), ("xbc", CONV_DIM),
                        ("dt", H_B), ("b", D_C), ("c", D_C), ("u", D_C)):
        cols[name] = (o, o + width)
        o += width

    def cs(name):
        lo, hi = cols[name]
        return w_in[:, :, lo:hi]

    pad_dt = jnp.zeros((depth, D_MODEL, 128 - H_B), w_in.dtype)
    w_in_r = jnp.concatenate(
        [cs("xbc"), cs("z"), cs("q"), cs("k"), cs("v"), cs("b"), cs("c"), cs("u"),
         cs("dt"), pad_dt], axis=2).astype(BF16)

    def lane_pad(v):
        return jnp.pad(v.astype(F32), ((0, 0), (0, 128 - H_B)))[:, None, :]

    def row(v):
        return v.astype(F32)[:, None, :]

    return dict(
        g_mix_pre=row(norm_mix_pre), g_mix_post=row(norm_mix_post),
        g_mlp_pre=row(norm_mlp_pre), g_mlp_post=row(norm_mlp_post),
        w_in=w_in_r,
        w_out_a=w_out[:, 0:D_A, :].astype(BF16),
        w_out_bc=w_out[:, D_A:, :].astype(BF16),
        attn_norm=row(attn_norm),
        ssd=(ssm_conv_w.astype(F32), row(ssm_conv_b), lane_pad(ssm_a_log),
             lane_pad(ssm_dt_bias), row(jnp.repeat(ssm_d, HEAD_DIM, axis=1)),
             row(ssm_norm), sconv_w.astype(F32), row(sconv_norm)),
        w_up=w_mlp_up.astype(BF16), w_dn=w_mlp_down.astype(BF16),
    )


def _run_trunk(x, states, w, prompt):
    batch, seq, _ = x.shape
    depth = w["w_in"].shape[0]
    m = batch * seq
    x2d = x.reshape(m, D_MODEL)
    tm = 512 if m % 512 == 0 else m
    outs = [[] for _ in range(5)]
    if prompt:
        chunk, nc = SSD_CHUNK, seq // SSD_CHUNK
        wb = min(WIN_MAX, seq)
    else:
        chunk, nc = seq, 1
        cache_k, cache_v, st_ssm, st_conv, st_sconv = states
        wb = cache_k.shape[2]
        cache_k = jnp.transpose(cache_k, (0, 1, 3, 4, 2)).reshape(depth, batch, D_A, wb)
        cache_v = jnp.transpose(cache_v, (0, 1, 3, 4, 2)).reshape(depth, batch, D_A, wb)
        new_k = jnp.zeros((depth, batch, D_A, wb), F32)
        new_v = jnp.zeros((depth, batch, D_A, wb), F32)
        h0 = st_ssm.astype(F32)
        cprev8 = jnp.pad(st_conv.astype(F32), ((0, 0), (0, 0), (8 - (CONV_W - 1), 0), (0, 0)))
        sprev8 = jnp.pad(st_sconv.astype(F32), ((0, 0), (0, 0), (8 - (SCONV_W - 1), 0), (0, 0)))
    for l in range(depth):
        if prompt:
            proj, qkv_perm = _in_proj(x2d, w["g_mix_pre"], w["w_in"], l, tm, seq)
            oa0, oa1 = _attn_prompt(qkv_perm, w["attn_norm"], l, batch, seq)
            p3 = proj.reshape(batch, seq, PROJ_W)
            new_k = p3[:, seq - wb:, COL_Q + D_A:COL_Q + 2 * D_A]
            new_v = p3[:, seq - wb:, COL_Q + 2 * D_A:COL_Q + 3 * D_A]
            x2d, h_new, c8, s8 = _fused(x2d, oa0, oa1, proj, w["w_out_a"], w["w_out_bc"],
                                        w["g_mix_post"], w["g_mlp_pre"], w["g_mlp_post"],
                                        w["w_up"], w["w_dn"], w["ssd"], l, batch, tm, 512)
        else:
            (proj,) = _in_proj(x2d, w["g_mix_pre"], w["w_in"], l, tm)
            oa0, oa1, new_k, new_v = _attn_sample(proj, cache_k, cache_v, w["attn_norm"],
                                                  new_k, new_v, l, batch, seq, wb)
            ybc, h_new, c8, s8 = _ssd(proj, h0, cprev8, sprev8, w["ssd"], l, batch, nc,
                                      chunk, SSD_GROUP if batch % SSD_GROUP == 0 else 1)
            ybc = ybc.reshape(m, D_B + D_C)
            x2d = _post(x2d, oa0, oa1, ybc, w["w_out_a"], w["w_out_bc"], w["g_mix_post"],
                        w["g_mlp_pre"], w["g_mlp_post"], w["w_up"], w["w_dn"], l, tm, 1024)
        if prompt:
            outs[0].append(new_k.reshape(batch, wb, H_A, HEAD_DIM))
            outs[1].append(new_v.reshape(batch, wb, H_A, HEAD_DIM))
        outs[2].append(h_new)
        outs[3].append(c8[:, 8 - (CONV_W - 1):, :])
        outs[4].append(s8[:, 8 - (SCONV_W - 1):, :])
    if not prompt:
        outs[0] = jnp.transpose(new_k.reshape(depth, batch, H_A, HEAD_DIM, wb), (0, 1, 4, 2, 3))
        outs[1] = jnp.transpose(new_v.reshape(depth, batch, H_A, HEAD_DIM, wb), (0, 1, 4, 2, 3))
    stacked = tuple(o if not isinstance(o, list) else jnp.stack(o) for o in outs)
    return x2d.reshape(batch, seq, D_MODEL), stacked


def kernel(x_prompt, x_sample, cache_attn_k, cache_attn_v, state_ssm, state_ssm_conv, state_sconv, norm_mix_pre, norm_mix_post, norm_mlp_pre, norm_mlp_post, w_in, w_out, attn_norm, ssm_conv_w, ssm_conv_b, ssm_a_log, ssm_dt_bias, ssm_d, ssm_norm, sconv_w, sconv_norm, w_mlp_up, w_mlp_down):
    w = _prep_weights(norm_mix_pre, norm_mix_post, norm_mlp_pre, norm_mlp_post, w_in, w_out,
                      attn_norm, ssm_conv_w, ssm_conv_b, ssm_a_log, ssm_dt_bias, ssm_d,
                      ssm_norm, sconv_w, sconv_norm, w_mlp_up, w_mlp_down)
    y_prompt, (p_k, p_v, p_ssm, p_conv, p_sconv) = _run_trunk(x_prompt, None, w, True)
    states = (cache_attn_k, cache_attn_v, state_ssm, state_ssm_conv, state_sconv)
    y_sample, (s_k, s_v, s_ssm, s_conv, s_sconv) = _run_trunk(x_sample, states, w, False)
    return (y_prompt, y_sample, p_k, p_v, p_ssm, p_conv, p_sconv,
            s_k, s_v, s_ssm, s_conv, s_sconv)
```

```python
import functools
import math

import jax
import jax.numpy as jnp
from jax import lax
from jax.experimental import pallas as pl
from jax.experimental.pallas import tpu as pltpu

F32 = jnp.float32
BF16 = jnp.bfloat16

HEAD_DIM = 64
D_MODEL = 1024
D_A = 256
H_A = 4
D_B = 512
H_B = 8
SSM_GROUPS = 2
N_STATE = 128
CONV_W = 4
CONV_DIM = D_B + 2 * SSM_GROUPS * N_STATE
D_C = 256
SCONV_W = 3
D_FF = 4 * D_MODEL
DILATED_CONFIGS = ((128, 1), (512, 4), (2048, 16))
WIN_MAX = 2048
EPS = 1e-6
LOG2E = 1.4426950408889634
ALIBI_SLOPES = tuple(2.0 ** (-8.0 * (h + 1) / H_A) for h in range(H_A))
ATTN_BLOCK = 128
ATTN_PERM = 16
ATTN_BLOCKS_PER_STEP = 16
ATTN_SAMPLE_GROUP = 2
SSD_CHUNK = 128
SSD_GROUP = 2
SSD_STAGES = 6

COL_XBC = 0
COL_Z = 1024
COL_Q = 1536
COL_G = 2304
COL_DT = 3072
PROJ_W = 3200

VMEM_LIMIT = 56 * 1024 * 1024


def _rms(x, g):
    return x * lax.rsqrt(jnp.mean(x * x, axis=-1, keepdims=True) + EPS) * g


def _silu(x):
    return x * (1.0 / (1.0 + jnp.exp(-x)))


def _softplus(x):
    return jnp.maximum(x, 0.0) + jnp.log1p(jnp.exp(-jnp.abs(x)))


def _in_proj_kernel(x_ref, g_ref, w_ref, *rest, permute):
    o_ref = rest[2] if permute else rest[0]
    h = _rms(x_ref[...], g_ref[...]).astype(BF16)
    for lo, hi in ((0, COL_Q), (COL_Q, COL_G), (COL_G, PROJ_W)):
        res = jnp.dot(h, w_ref[:, lo:hi], preferred_element_type=F32)
        o_ref[:, lo:hi] = res
        if permute and lo == COL_Q:
            operm_ref, pkt_ref, pvt_ref, scr_ref = rest[3:]
            tm = x_ref.shape[0]
            pkt_ref[...] = res[:, D_A:2 * D_A].T
            pvt_ref[...] = res[:, 2 * D_A:3 * D_A].T
            for j in range(3 * D_A // 128):
                scr_ref[j] = res[:, j * 128:(j + 1) * 128]
            for r in range(ATTN_PERM):
                for j in range(3 * D_A // 128):
                    operm_ref[r, :, j * 128:(j + 1) * 128] = (
                        scr_ref.at[j][pl.ds(r, tm // ATTN_PERM, stride=ATTN_PERM), :])


def _in_proj(x2d, g, w, layer, tm, seq=None, wb=None, pkt=None, pvt=None):
    m = x2d.shape[0]
    permute = seq is not None
    in_specs = [
        pl.BlockSpec((tm, D_MODEL), lambda i: (i, 0)),
        pl.BlockSpec((None, 1, D_MODEL), lambda i: (layer, 0, 0)),
        pl.BlockSpec((None, D_MODEL, PROJ_W), lambda i: (layer, 0, 0),
                     pipeline_mode=pl.Buffered(1)),
    ]
    args = [x2d, g, w]
    out_specs = [pl.BlockSpec((tm, PROJ_W), lambda i: (i, 0))]
    out_shape = [jax.ShapeDtypeStruct((m, PROJ_W), F32)]
    scratch = []
    aliases = {}
    if permute:
        tiles = seq // tm
        rows = tm // ATTN_PERM
        first = (seq - wb) // tm
        out_specs.append(pl.BlockSpec((None, ATTN_PERM, rows, 3 * D_A),
                                      lambda i: (i // tiles, 0, i % tiles, 0)))
        out_shape.append(jax.ShapeDtypeStruct((m // seq, ATTN_PERM, seq // ATTN_PERM, 3 * D_A), F32))
        win_spec = pl.BlockSpec((None, None, D_A, tm),
                                lambda i: (layer, i // tiles, 0, jnp.maximum(i % tiles - first, 0)))
        out_specs += [win_spec, win_spec]
        out_shape += [jax.ShapeDtypeStruct(pkt.shape, F32)] * 2
        in_specs += [pl.BlockSpec(memory_space=pl.ANY)] * 2
        args += [pkt, pvt]
        aliases = {3: 2, 4: 3}
        scratch.append(pltpu.VMEM((3 * D_A // 128, tm, 128), F32))
    return pl.pallas_call(
        functools.partial(_in_proj_kernel, permute=permute),
        grid=(m // tm,),
        in_specs=in_specs,
        out_specs=out_specs,
        out_shape=out_shape,
        input_output_aliases=aliases,
        scratch_shapes=scratch,
        compiler_params=pltpu.CompilerParams(
            dimension_semantics=("arbitrary",), vmem_limit_bytes=VMEM_LIMIT),
        name="in_proj",
    )(*args)


def _attn_prompt_kernel(q_ref, k_ref, v_ref, g_ref, o0_ref, o1_ref,
                        acc_ref, m_ref, l_ref, bias_ref, *, seq):
    blk = ATTN_BLOCK
    nbr = len(DILATED_CONFIGS)
    P = ATTN_PERM
    per_res = seq // P

    def local_to_strided(a, dil):
        nchunk = P // dil
        rows = blk // nchunk
        return nchunk * (a % rows) + a // rows

    shape4 = (H_A * blk, 2 * blk)
    row4 = lax.broadcasted_iota(jnp.int32, shape4, 0)
    ki = lax.broadcasted_iota(jnp.int32, shape4, 1)
    slope = jnp.zeros(shape4, F32)
    for h in range(H_A):
        slope = jnp.where(row4 // blk == h, ALIBI_SLOPES[h], slope)
    for bi, (win, dil) in enumerate(DILATED_CONFIGS):
        assert win // dil == blk and P % dil == 0
        jq = blk + local_to_strided(row4 % blk, dil)
        jk = local_to_strided(ki % blk, dil) + blk * (ki // blk)
        diff = jq - jk
        band = (diff >= 0) & (diff <= blk)
        dist = slope * (-float(dil) * LOG2E * diff.astype(F32))
        bias_ref[bi] = jnp.where(band, dist, -jnp.inf)
        bias_ref[nbr + bi] = jnp.where(band & (ki >= blk), dist, -jnp.inf)

    lane_head = lax.broadcasted_iota(jnp.int32, (blk, D_A), 1) // HEAD_DIM
    lane_lo = lax.broadcasted_iota(jnp.int32, (blk, 128), 1) < HEAD_DIM

    def per_head(col):
        c = [col[h * blk:(h + 1) * blk, :] for h in range(H_A)]
        return jnp.concatenate([jnp.where(lane_lo, c[0], c[1]), jnp.where(lane_lo, c[2], c[3])],
                               axis=1)

    def blocks(r, u, bi, dil):
        nchunk = P // dil
        rows = blk // nchunk
        nb = min(ATTN_BLOCKS_PER_STEP, per_res // rows)
        starts = [pl.multiple_of(jnp.maximum(nb * u - 1 + j, 0) * rows, rows)
                  for j in range(nb + 1)]

        def load(ref, st):
            return jnp.concatenate(
                [ref[dil * c + r, pl.ds(st, rows), :] for c in range(nchunk)], axis=0)

        def store(ref, st, val):
            for c in range(nchunk):
                ref[dil * c + r, pl.ds(st, rows), :] = val[c * rows:(c + 1) * rows, :]

        kb = [load(k_ref, st).astype(BF16) for st in starts]
        vb = [load(v_ref, st).astype(BF16) for st in starts]
        results = []
        for j in range(nb):
            q = (load(q_ref, starts[1 + j]) * (HEAD_DIM ** -0.5 * LOG2E)).astype(BF16)
            q4 = jnp.concatenate(
                [jnp.where(lane_head == h, q, jnp.zeros_like(q)) for h in range(H_A)], axis=0)
            kk = jnp.concatenate([kb[j], kb[j + 1]], axis=0)
            vv = jnp.concatenate([vb[j], vb[j + 1]], axis=0)
            s = lax.dot_general(q4, kk, (((1,), (1,)), ((), ())), preferred_element_type=F32)
            if j == 0:
                s = s + bias_ref[jnp.where(u == 0, nbr + bi, bi)]
            else:
                s = s + bias_ref[bi]
            m4 = jnp.max(s, axis=-1, keepdims=True)
            p = jnp.exp2(s - m4)
            l4 = jnp.sum(p, axis=-1, keepdims=True)
            pb = p.astype(BF16)
            o = [jnp.dot(pb[h * blk:(h + 1) * blk, :],
                         vv[:, (h // 2) * 128:(h // 2 + 1) * 128],
                         preferred_element_type=F32) for h in range(H_A)]
            acc_b = jnp.concatenate([jnp.where(lane_lo, o[0], o[1]),
                                     jnp.where(lane_lo, o[2], o[3])], axis=1)
            m_b = per_head(m4)
            l_b = per_head(l4)
            if bi > 0:
                st = starts[1 + j]
                m_old = load(m_ref, st)
                m_new = jnp.maximum(m_old, m_b)
                a_old = jnp.exp2(m_old - m_new)
                a_b = jnp.exp2(m_b - m_new)
                acc_b = load(acc_ref, st) * a_old + acc_b * a_b
                l_b = load(l_ref, st) * a_old + l_b * a_b
                m_b = m_new
            results.append((acc_b, m_b, l_b))
        for j in range(nb):
            store(acc_ref, starts[1 + j], results[j][0])
            store(m_ref, starts[1 + j], results[j][1])
            store(l_ref, starts[1 + j], results[j][2])

    for bi, (win, dil) in enumerate(DILATED_CONFIGS):
        nblk = per_res // (blk // (P // dil))
        nb = min(ATTN_BLOCKS_PER_STEP, nblk)
        nstep = nblk // nb
        nres = ATTN_BLOCKS_PER_STEP // nb
        assert dil % nres == 0

        def body(it, carry, bi=bi, dil=dil, nstep=nstep, nres=nres):
            rg = it // nstep
            u = it - rg * nstep
            for rr in range(nres):
                blocks(rg * nres + rr, u, bi, dil)
            return carry

        lax.fori_loop(0, (dil // nres) * nstep, body, 0)

    def finish(r, carry):
        o = _rms(acc_ref[r] / l_ref[r], g_ref[...])
        o0_ref[pl.ds(r, per_res, stride=P), :] = o[:, 0:128]
        o1_ref[pl.ds(r, per_res, stride=P), :] = o[:, 128:256]
        return carry

    lax.fori_loop(0, P, finish, 0)


def _attn_prompt(qkv_perm, g, layer, batch, seq):
    per_res = seq // ATTN_PERM
    qkv_specs = [pl.BlockSpec((None, ATTN_PERM, per_res, D_A),
                              functools.partial(lambda b, j: (b, 0, 0, j), j=j)) for j in range(3)]
    half_spec = pl.BlockSpec((seq, 128), lambda b: (b, 0))
    return pl.pallas_call(
        functools.partial(_attn_prompt_kernel, seq=seq),
        grid=(batch,),
        in_specs=qkv_specs + [pl.BlockSpec((None, 1, D_A), lambda b: (layer, 0, 0))],
        out_specs=[half_spec, half_spec],
        out_shape=[jax.ShapeDtypeStruct((batch * seq, 128), F32)] * 2,
        scratch_shapes=[pltpu.VMEM((ATTN_PERM, per_res, D_A), F32)] * 3
        + [pltpu.VMEM((2 * len(DILATED_CONFIGS), H_A * ATTN_BLOCK, 2 * ATTN_BLOCK), F32)],
        compiler_params=pltpu.CompilerParams(
            dimension_semantics=("parallel",), vmem_limit_bytes=VMEM_LIMIT),
        name="attn_prompt",
    )(qkv_perm, qkv_perm, qkv_perm, g)


def _attn_sample_kernel(q_ref, k_ref, v_ref, ckt_ref, cvt_ref, g_ref, nk_in_ref, nv_in_ref,
                        o0_ref, o1_ref, nkt_ref, nvt_ref, kpad_ref, vpad_ref, *, wb, s_len, group):
    pad = 128
    rows = H_A * s_len
    ncol = wb + pad
    row_i = lax.broadcasted_iota(jnp.int32, (rows, ncol), 0)
    col_i = lax.broadcasted_iota(jnp.int32, (rows, ncol), 1)
    s_i = row_i % s_len
    pos = jnp.where(col_i < wb, col_i, col_i + s_len - pad)
    d = wb + s_i - pos
    real = (col_i < wb) | (col_i >= ncol - s_len)
    mult = jnp.zeros((rows, ncol), F32)
    for win, dil in DILATED_CONFIGS:
        hit = real & (d >= 0) & (d <= win) & ((d % dil) == 0)
        mult = mult + jnp.where(hit, 1.0, 0.0)
    slope = jnp.zeros((rows, ncol), F32)
    for h in range(H_A):
        slope = jnp.where(row_i // s_len == h, ALIBI_SLOPES[h], slope)
    bias = jnp.where(mult > 0.0, -slope * d.astype(F32), -jnp.inf)
    tail = lax.broadcasted_iota(jnp.int32, (D_A, pad), 1) >= pad - s_len
    lane_head = lax.broadcasted_iota(jnp.int32, (s_len, D_A), 1) // HEAD_DIM
    nt = (((1,), (1,)), ((), ()))

    for i in range(group):
        tok = slice(i * s_len, (i + 1) * s_len)
        k_new = k_ref[tok, :]
        v_new = v_ref[tok, :]
        kpad_ref[i] = jnp.zeros(kpad_ref.shape[1:], F32)
        vpad_ref[i] = jnp.zeros(vpad_ref.shape[1:], F32)
        kpad_ref[i, pad - s_len:pad, :] = k_new
        vpad_ref[i, pad - s_len:pad, :] = v_new
        kpad = kpad_ref[i]
        vpad = vpad_ref[i]
        kpad_t = kpad.T
        vpad_t = vpad.T

        for src_ref, dst_ref, new_t in ((ckt_ref, nkt_ref, kpad_t), (cvt_ref, nvt_ref, vpad_t)):
            rolled = pltpu.roll(src_ref[i], wb - s_len, axis=1)
            dst_ref[i, :, 0:wb - pad] = rolled[:, 0:wb - pad]
            dst_ref[i, :, wb - pad:wb] = jnp.where(tail, new_t, rolled[:, wb - pad:wb])

        q = q_ref[tok, :] * (HEAD_DIM ** -0.5)
        qh = jnp.concatenate([jnp.where(lane_head == h, q, 0.0) for h in range(H_A)],
                             axis=0).astype(BF16)
        s1 = jnp.dot(qh, ckt_ref[i].astype(BF16), preferred_element_type=F32)
        s2 = jnp.dot(qh, kpad_t.astype(BF16), preferred_element_type=F32)
        s = jnp.concatenate([s1, s2], axis=1) + bias
        m = jnp.max(s, axis=-1, keepdims=True)
        p = jnp.exp(s - m) * mult
        l = jnp.sum(p, axis=-1, keepdims=True)
        pb = p.astype(BF16)
        o = lax.dot_general(pb[:, 0:wb], cvt_ref[i].astype(BF16), nt, preferred_element_type=F32)
        o = o + jnp.dot(pb[:, wb:ncol], vpad.astype(BF16), preferred_element_type=F32)
        o = o / l
        out = jnp.zeros((s_len, D_A), F32)
        for h in range(H_A):
            out = jnp.where(lane_head == h, o[h * s_len:(h + 1) * s_len, :], out)
        out = _rms(out, g_ref[...])
        o0_ref[tok, :] = out[:, 0:128]
        o1_ref[tok, :] = out[:, 128:256]


def _attn_sample(proj, cache_k, cache_v, g, new_k, new_v, layer, batch, s_len, wb, group):
    depth = cache_k.shape[0]
    qb = COL_Q // D_A
    cache_spec = pl.BlockSpec((None, group, D_A, wb), lambda b: (layer, b, 0, 0))
    rows = group * s_len
    return pl.pallas_call(
        functools.partial(_attn_sample_kernel, wb=wb, s_len=s_len, group=group),
        grid=(batch // group,),
        in_specs=[
            pl.BlockSpec((rows, D_A), lambda b: (b, qb)),
            pl.BlockSpec((rows, D_A), lambda b: (b, qb + 1)),
            pl.BlockSpec((rows, D_A), lambda b: (b, qb + 2)),
            cache_spec, cache_spec,
            pl.BlockSpec((None, 1, D_A), lambda b: (layer, 0, 0)),
            pl.BlockSpec(memory_space=pl.ANY), pl.BlockSpec(memory_space=pl.ANY),
        ],
        out_specs=[pl.BlockSpec((rows, 128), lambda b: (b, 0))] * 2 + [cache_spec, cache_spec],
        out_shape=[jax.ShapeDtypeStruct((batch * s_len, 128), F32)] * 2 + [
                   jax.ShapeDtypeStruct((depth, batch, D_A, wb), F32)] * 2,
        input_output_aliases={6: 2, 7: 3},
        scratch_shapes=[pltpu.VMEM((group, 128, D_A), F32)] * 2,
        compiler_params=pltpu.CompilerParams(
            dimension_semantics=("parallel",), vmem_limit_bytes=VMEM_LIMIT),
        name="attn_sample",
    )(proj, proj, proj, cache_k, cache_v, g, new_k, new_v)


def _ssd_kernel(xbc_ref, z_ref, gate_ref, dtr_ref, h0_ref, cprev_ref, sprev_ref,
                cw_ref, cb_ref, alog_ref, dtb_ref, dsk_ref, sn_ref, scw_ref, scn_ref,
                hall_ref, call_ref, sall_ref,
                y_ref, hout_ref, cnew_ref, snew_ref,
                ext_ref, ext2_ref, h_ref, *, chunk, group):
    c = pl.program_id(1)
    nc = pl.num_programs(1)

    @pl.when(c == 0)
    def _init():
        h_ref[...] = h0_ref[...]
        ext_ref[:, 0:8, :] = cprev_ref[...]
        ext2_ref[:, 0:8, :] = sprev_ref[...]

    for i in range(group):
        _ssd_one(xbc_ref.at[i], z_ref.at[i], gate_ref.at[i], dtr_ref.at[i],
                 cw_ref, cb_ref, alog_ref, dtb_ref, dsk_ref, sn_ref, scw_ref, scn_ref,
                 y_ref.at[i], ext_ref.at[i], ext2_ref.at[i], h_ref.at[i], chunk=chunk)

    @pl.when(c == nc - 1)
    def _final():
        hout_ref[...] = h_ref[...]
        cnew_ref[...] = ext_ref[:, chunk:chunk + 8, :]
        snew_ref[...] = ext2_ref[:, chunk:chunk + 8, :]

    ext_ref[:, 0:8, :] = ext_ref[:, chunk:chunk + 8, :]
    ext2_ref[:, 0:8, :] = ext2_ref[:, chunk:chunk + 8, :]


def _ssd_one(*args, **kwargs):
    for _ in _ssd_stages(*args, **kwargs):
        pass


def _ssd_stages(xbc_ref, z_ref, gate_ref, dtr_ref,
                cw_ref, cb_ref, alog_ref, dtb_ref, dsk_ref, sn_ref, scw_ref, scn_ref,
                y_ref, ext_ref, ext2_ref, h_ref, *, chunk):
    C = chunk
    heads_per_group = H_B // SSM_GROUPS
    gw = heads_per_group * HEAD_DIM

    ext_ref[8:8 + C, :] = xbc_ref[...]
    xc_parts = []
    half = CONV_DIM // 2
    for lo in (0, half):
        conv = xbc_ref[:, lo:lo + half] * cw_ref[CONV_W - 1:CONV_W, lo:lo + half]
        for k in range(CONV_W - 1):
            off = 8 - (CONV_W - 1) + k
            conv = conv + ext_ref[off:off + C, lo:lo + half] * cw_ref[k:k + 1, lo:lo + half]
        xc_parts.append(_silu(conv + cb_ref[:, lo:lo + half]))
        yield
    x_s = xc_parts[0]
    bm = xc_parts[1][:, 0:SSM_GROUPS * N_STATE]
    cm = xc_parts[1][:, SSM_GROUPS * N_STATE:2 * SSM_GROUPS * N_STATE]

    gates = gate_ref[...]
    b_gate = gates[:, 0:D_C]
    prod = gates[:, D_C:2 * D_C] * gates[:, 2 * D_C:3 * D_C]
    ext2_ref[8:8 + C, :] = prod
    sconv = prod * scw_ref[SCONV_W - 1:SCONV_W, :]
    for k in range(SCONV_W - 1):
        off = 8 - (SCONV_W - 1) + k
        sconv = sconv + ext2_ref[off:off + C, :] * scw_ref[k:k + 1, :]
    y_ref[:, D_B:D_B + D_C] = _rms(b_gate * sconv, scn_ref[...]).astype(y_ref.dtype)

    row = lax.broadcasted_iota(jnp.int32, (C, C), 0)
    col = lax.broadcasted_iota(jnp.int32, (C, C), 1)
    causal = row >= col
    dt8 = _softplus((dtr_ref[...] + dtb_ref[...]).T[0:H_B, :])
    a8 = jnp.broadcast_to(-jnp.exp(alog_ref[...]), (128, 128)).T[0:H_B, 0:C]
    da = dt8 * a8
    upper = jnp.where(row <= col, 1.0, 0.0).astype(BF16)
    da_hi = da.astype(BF16)
    rem = da - da_hi.astype(F32)
    da_mid = rem.astype(BF16)
    da_lo = (rem - da_mid.astype(F32)).astype(BF16)
    acs8 = (jnp.dot(da_hi, upper, preferred_element_type=F32)
            + jnp.dot(da_mid, upper, preferred_element_type=F32)
            + jnp.dot(da_lo, upper, preferred_element_type=F32))
    last8 = acs8[:, C - 1:C]
    cdec8 = jnp.exp(last8)
    stacked = jnp.concatenate(
        [acs8, jnp.exp(acs8), jnp.exp(last8 - acs8) * dt8, jnp.zeros((128 - 3 * H_B, C), F32)],
        axis=0)
    cols = stacked.T
    lane_lo = lax.broadcasted_iota(jnp.int32, (C, 2 * HEAD_DIM), 1) < HEAD_DIM

    def expand(base):
        return jnp.concatenate(
            [jnp.where(lane_lo, cols[:, base + 2 * p:base + 2 * p + 1],
                       cols[:, base + 2 * p + 1:base + 2 * p + 2]) for p in range(H_B // 2)],
            axis=1)

    e_acs_x = expand(H_B)
    w_end_x = expand(2 * H_B)
    yield

    nt = (((1,), (1,)), ((), ()))
    tn = (((0,), (0,)), ((), ()))
    x_bf = x_s.astype(BF16)
    xw = (x_s * w_end_x)
    y_parts = []
    for g in range(SSM_GROUPS):
        bm_g = bm[:, g * N_STATE:(g + 1) * N_STATE].astype(BF16)
        cm_g = cm[:, g * N_STATE:(g + 1) * N_STATE].astype(BF16)
        scores = lax.dot_general(cm_g, bm_g, nt, preferred_element_type=F32)
        h_g = h_ref[g * heads_per_group:(g + 1) * heads_per_group].reshape(gw, N_STATE)
        y_off = lax.dot_general(cm_g, h_g.astype(BF16), nt, preferred_element_type=F32)
        y_off = y_off * e_acs_x[:, g * gw:(g + 1) * gw]
        diag = []
        for pair in range(heads_per_group // 2):
            outs = []
            for hh in range(2):
                h = g * heads_per_group + 2 * pair + hh
                seg = cols[:, h:h + 1] - acs8[h:h + 1, :]
                decay = jnp.exp(jnp.where(causal, seg, -jnp.inf))
                mh = (scores * decay * dt8[h:h + 1, :]).astype(BF16)
                lo = (g * heads_per_group + 2 * pair) * HEAD_DIM
                outs.append(jnp.dot(mh, x_bf[:, lo:lo + 2 * HEAD_DIM],
                                    preferred_element_type=F32))
            diag.append(jnp.where(lane_lo, outs[0], outs[1]))
        y_parts.append(jnp.concatenate(diag, axis=1) + y_off)
        st = lax.dot_general(xw[:, g * gw:(g + 1) * gw].astype(BF16), bm_g, tn,
                             preferred_element_type=F32)
        for hh in range(heads_per_group):
            h = g * heads_per_group + hh
            h_ref[h] = h_ref[h] * cdec8[h:h + 1, 0:1] + st[hh * HEAD_DIM:(hh + 1) * HEAD_DIM, :]
        yield
    y = jnp.concatenate(y_parts, axis=1) + dsk_ref[...] * x_s
    y_ref[:, 0:D_B] = _rms(y * _silu(z_ref[...]), sn_ref[...]).astype(y_ref.dtype)
    yield


def _ssd(proj, h0, cprev8, sprev8, wts, outs, layer, batch, nc, chunk, group):
    (cw, cb, alog, dtb, dsk, sn, scw, scn) = wts
    proj3 = proj.reshape(batch, nc * chunk, PROJ_W)

    def tok(width, blk_idx):
        return pl.BlockSpec((group, chunk, width), lambda b, c: (b, c, blk_idx))

    def per_layer(shape):
        return pl.BlockSpec((None,) + shape, lambda b, c: (layer,) + (0,) * len(shape))

    state_spec = pl.BlockSpec((group, H_B, HEAD_DIM, N_STATE), lambda b, c: (b, 0, 0, 0))
    conv_spec = pl.BlockSpec((group, 8, CONV_DIM), lambda b, c: (b, 0, 0))
    sconv_spec = pl.BlockSpec((group, 8, D_C), lambda b, c: (b, 0, 0))

    def stacked(spec):
        return pl.BlockSpec((None,) + tuple(spec.block_shape),
                            lambda b, c: (layer,) + tuple(spec.index_map(b, c)))

    return pl.pallas_call(
        functools.partial(_ssd_kernel, chunk=chunk, group=group),
        grid=(batch // group, nc),
        in_specs=[
            tok(CONV_DIM, COL_XBC // CONV_DIM),
            tok(D_B, COL_Z // D_B),
            tok(3 * D_C, COL_G // (3 * D_C)),
            tok(128, COL_DT // 128),
            stacked(state_spec), stacked(conv_spec), stacked(sconv_spec),
            per_layer((CONV_W, CONV_DIM)), per_layer((1, CONV_DIM)),
            per_layer((1, 128)), per_layer((1, 128)), per_layer((1, D_B)),
            per_layer((1, D_B)), per_layer((SCONV_W, D_C)), per_layer((1, D_C)),
        ] + [pl.BlockSpec(memory_space=pl.ANY)] * 3,
        out_specs=[
            pl.BlockSpec((group, chunk, D_B + D_C), lambda b, c: (b, c, 0)),
            stacked(state_spec), stacked(conv_spec), stacked(sconv_spec),
        ],
        out_shape=[jax.ShapeDtypeStruct((batch, nc * chunk, D_B + D_C), BF16)]
        + [jax.ShapeDtypeStruct(o.shape, F32) for o in outs],
        input_output_aliases={15: 1, 16: 2, 17: 3},
        scratch_shapes=[
            pltpu.VMEM((group, chunk + 8, CONV_DIM), F32),
            pltpu.VMEM((group, chunk + 8, D_C), F32),
            pltpu.VMEM((group, H_B, HEAD_DIM, N_STATE), F32),
        ],
        compiler_params=pltpu.CompilerParams(
            dimension_semantics=("parallel", "arbitrary"), vmem_limit_bytes=VMEM_LIMIT),
        name="ssd_sconv",
    )(proj3, proj3, proj3, proj3, h0, cprev8, sprev8, cw, cb, alog, dtb, dsk, sn, scw, scn, *outs)


def _post_kernel(x_ref, oa0_ref, oa1_ref, ybc_ref, woa_ref, wobc_ref, gpost_ref, gpre_ref,
                 gmlp_ref, wup_ref, wdn_ref, o_ref, acc_ref, *, tf):
    oa = jnp.concatenate([oa0_ref[...], oa1_ref[...]], axis=1).astype(BF16)
    mix = jnp.dot(oa, woa_ref[...], preferred_element_type=F32)
    mix = mix + jnp.dot(ybc_ref[...], wobc_ref[...], preferred_element_type=F32)
    x1 = x_ref[...] + _rms(mix, gpost_ref[...])
    h = _rms(x1, gpre_ref[...]).astype(BF16)
    for j in range(D_FF // tf):
        u = jnp.dot(h, wup_ref[:, j * tf:(j + 1) * tf], preferred_element_type=F32)
        u = jnp.square(jnp.maximum(u, 0.0)).astype(BF16)
        f = jnp.dot(u, wdn_ref[j * tf:(j + 1) * tf, :], preferred_element_type=F32)
        if j == 0:
            acc_ref[...] = f
        else:
            acc_ref[...] += f
    o_ref[...] = x1 + _rms(acc_ref[...], gmlp_ref[...])


def _post(x2d, oa0, oa1, ybc, w_out_a, w_out_bc, gpost, gpre, gmlp, w_up, w_dn, layer, tm, tf):
    m = x2d.shape[0]

    def resident(shape):
        return pl.BlockSpec((None,) + shape, lambda i: (layer, 0, 0),
                            pipeline_mode=pl.Buffered(1))

    return pl.pallas_call(
        functools.partial(_post_kernel, tf=tf),
        grid=(m // tm,),
        in_specs=[
            pl.BlockSpec((tm, D_MODEL), lambda i: (i, 0)),
            pl.BlockSpec((tm, 128), lambda i: (i, 0)),
            pl.BlockSpec((tm, 128), lambda i: (i, 0)),
            pl.BlockSpec((tm, D_B + D_C), lambda i: (i, 0)),
            resident((D_A, D_MODEL)), resident((D_B + D_C, D_MODEL)),
            resident((1, D_MODEL)), resident((1, D_MODEL)), resident((1, D_MODEL)),
            resident((D_MODEL, D_FF)), resident((D_FF, D_MODEL)),
        ],
        out_specs=pl.BlockSpec((tm, D_MODEL), lambda i: (i, 0)),
        out_shape=jax.ShapeDtypeStruct((m, D_MODEL), F32),
        scratch_shapes=[pltpu.VMEM((tm, D_MODEL), F32)],
        compiler_params=pltpu.CompilerParams(
            dimension_semantics=("parallel",), vmem_limit_bytes=VMEM_LIMIT),
        name="post",
    )(x2d, oa0, oa1, ybc, w_out_a, w_out_bc, gpost, gpre, gmlp, w_up, w_dn)


def _fused_kernel(x_ref, oa0_ref, oa1_ref, woa_ref, wobc_ref, gpost_ref, gpre_ref, gmlp_ref,
                  wup_ref, wdn_ref,
                  xbc_ref, z_ref, gate_ref, dtr_ref,
                  cw_ref, cb_ref, alog_ref, dtb_ref, dsk_ref, sn_ref, scw_ref, scn_ref,
                  hall_ref, call_ref, sall_ref,
                  o_ref, hout_ref, cnew_ref, snew_ref,
                  acc_ref, ybc_ref, ext_ref, ext2_ref, h_ref, *, tf, tm, tiles_per_seq, ntiles):
    i = pl.program_id(0)
    slot = i % 2
    nchunks = tm // SSD_CHUNK

    @pl.when(i == 0)
    def _first():
        ybc_ref[...] = jnp.zeros(ybc_ref.shape, ybc_ref.dtype)

    @pl.when(i % tiles_per_seq == 0)
    def _new_sequence():
        h_ref[...] = jnp.zeros(h_ref.shape, F32)
        ext_ref[0:8, :] = jnp.zeros((8, CONV_DIM), F32)
        ext2_ref[0:8, :] = jnp.zeros((8, D_C), F32)

    def ssd_chunk(c):
        rows = slice(c * SSD_CHUNK, (c + 1) * SSD_CHUNK)
        yield from _ssd_stages(
            xbc_ref.at[rows], z_ref.at[rows], gate_ref.at[rows], dtr_ref.at[rows],
            cw_ref, cb_ref, alog_ref, dtb_ref, dsk_ref, sn_ref, scw_ref, scn_ref,
            ybc_ref.at[slot, rows], ext_ref, ext2_ref, h_ref, chunk=SSD_CHUNK)
        ext_ref[0:8, :] = ext_ref[SSD_CHUNK:SSD_CHUNK + 8, :]
        ext2_ref[0:8, :] = ext2_ref[SSD_CHUNK:SSD_CHUNK + 8, :]

    def ssd_tile():
        for c in range(nchunks):
            yield from ssd_chunk(c)

    stages = ssd_tile()
    nff = D_FF // tf
    per_ff = -(-(nchunks * SSD_STAGES) // (nff + 1))

    def advance():
        for _ in range(per_ff):
            next(stages, None)

    oa = jnp.concatenate([oa0_ref[...], oa1_ref[...]], axis=1).astype(BF16)
    mix = jnp.dot(oa, woa_ref[...], preferred_element_type=F32)
    mix = mix + jnp.dot(ybc_ref[1 - slot], wobc_ref[...], preferred_element_type=F32)
    x1 = x_ref[...] + _rms(mix, gpost_ref[...])
    h = _rms(x1, gpre_ref[...]).astype(BF16)
    advance()
    for j in range(nff):
        u = jnp.dot(h, wup_ref[:, j * tf:(j + 1) * tf], preferred_element_type=F32)
        u = jnp.square(jnp.maximum(u, 0.0)).astype(BF16)
        f = jnp.dot(u, wdn_ref[j * tf:(j + 1) * tf, :], preferred_element_type=F32)
        if j == 0:
            acc_ref[...] = f
        else:
            acc_ref[...] += f
        advance()
    for _ in stages:
        pass
    o_ref[...] = x1 + _rms(acc_ref[...], gmlp_ref[...])

    @pl.when((i % tiles_per_seq == tiles_per_seq - 1) & (i < ntiles))
    def _sequence_done():
        hout_ref[...] = h_ref[...]
        cnew_ref[...] = ext_ref[0:8, :]
        snew_ref[...] = ext2_ref[0:8, :]


def _fused(x2d, oa0, oa1, proj, w_out_a, w_out_bc, gpost, gpre, gmlp, w_up, w_dn, wts, outs,
           layer, batch, tm, tf):
    m = x2d.shape[0]
    (cw, cb, alog, dtb, dsk, sn, scw, scn) = wts
    ntiles = m // tm
    tps = ntiles // batch

    def resident(shape):
        return pl.BlockSpec((None,) + shape, lambda i: (layer,) + (0,) * len(shape),
                            pipeline_mode=pl.Buffered(1))

    def cur(width, blk_idx):
        return pl.BlockSpec((tm, width), lambda i: (jnp.minimum(i, ntiles - 1), blk_idx))

    def prev(width):
        return pl.BlockSpec((tm, width), lambda i: (jnp.maximum(i - 1, 0), 0))

    def per_seq(shape):
        return pl.BlockSpec((None, None) + shape,
                            lambda i: (layer, jnp.minimum(i, ntiles - 1) // tps) + (0,) * len(shape))

    return pl.pallas_call(
        functools.partial(_fused_kernel, tf=tf, tm=tm, tiles_per_seq=tps, ntiles=ntiles),
        grid=(ntiles + 1,),
        in_specs=[
            prev(D_MODEL), prev(128), prev(128),
            resident((D_A, D_MODEL)), resident((D_B + D_C, D_MODEL)),
            resident((1, D_MODEL)), resident((1, D_MODEL)), resident((1, D_MODEL)),
            resident((D_MODEL, D_FF)), resident((D_FF, D_MODEL)),
            cur(CONV_DIM, COL_XBC // CONV_DIM), cur(D_B, COL_Z // D_B),
            cur(3 * D_C, COL_G // (3 * D_C)), cur(128, COL_DT // 128),
            resident((CONV_W, CONV_DIM)), resident((1, CONV_DIM)),
            resident((1, 128)), resident((1, 128)), resident((1, D_B)),
            resident((1, D_B)), resident((SCONV_W, D_C)), resident((1, D_C)),
        ] + [pl.BlockSpec(memory_space=pl.ANY)] * 3,
        out_specs=[prev(D_MODEL), per_seq((H_B, HEAD_DIM, N_STATE)), per_seq((8, CONV_DIM)),
                   per_seq((8, D_C))],
        out_shape=[jax.ShapeDtypeStruct((m, D_MODEL), F32)]
        + [jax.ShapeDtypeStruct(o.shape, F32) for o in outs],
        input_output_aliases={22: 1, 23: 2, 24: 3},
        scratch_shapes=[pltpu.VMEM((tm, D_MODEL), F32),
                        pltpu.VMEM((2, tm, D_B + D_C), BF16),
                        pltpu.VMEM((SSD_CHUNK + 8, CONV_DIM), F32),
                        pltpu.VMEM((SSD_CHUNK + 8, D_C), F32),
                        pltpu.VMEM((H_B, HEAD_DIM, N_STATE), F32)],
        compiler_params=pltpu.CompilerParams(
            dimension_semantics=("arbitrary",), vmem_limit_bytes=VMEM_LIMIT),
        name="ssd_post",
    )(x2d, oa0, oa1, w_out_a, w_out_bc, gpost, gpre, gmlp, w_up, w_dn,
      proj, proj, proj, proj, cw, cb, alog, dtb, dsk, sn, scw, scn, *outs)


def _prep_weights(norm_mix_pre, norm_mix_post, norm_mlp_pre, norm_mlp_post, w_in, w_out,
                  attn_norm, ssm_conv_w, ssm_conv_b, ssm_a_log, ssm_dt_bias, ssm_d, ssm_norm,
                  sconv_w, sconv_norm, w_mlp_up, w_mlp_down):
    depth = w_in.shape[0]
    o = 0
    cols = {}
    for name, width in (("q", D_A), ("k", D_A), ("v", D_A), ("z", D_B), ("xbc", CONV_DIM),
                        ("dt", H_B), ("b", D_C), ("c", D_C), ("u", D_C)):
        cols[name] = (o, o + width)
        o += width

    def cs(name):
        lo, hi = cols[name]
        return w_in[:, :, lo:hi]

    pad_dt = jnp.zeros((depth, D_MODEL, 128 - H_B), w_in.dtype)
    w_in_r = jnp.concatenate(
        [cs("xbc"), cs("z"), cs("q"), cs("k"), cs("v"), cs("b"), cs("c"), cs("u"),
         cs("dt"), pad_dt], axis=2).astype(BF16)

    def lane_pad(v):
        return jnp.pad(v.astype(F32), ((0, 0), (0, 128 - H_B)))[:, None, :]

    def row(v):
        return v.astype(F32)[:, None, :]

    return dict(
        g_mix_pre=row(norm_mix_pre), g_mix_post=row(norm_mix_post),
        g_mlp_pre=row(norm_mlp_pre), g_mlp_post=row(norm_mlp_post),
        w_in=w_in_r,
        w_out_a=w_out[:, 0:D_A, :].astype(BF16),
        w_out_bc=w_out[:, D_A:, :].astype(BF16),
        attn_norm=row(attn_norm),
        ssd=(ssm_conv_w.astype(F32), row(ssm_conv_b), lane_pad(ssm_a_log),
             lane_pad(ssm_dt_bias), row(jnp.repeat(ssm_d, HEAD_DIM, axis=1)),
             row(ssm_norm), sconv_w.astype(F32), row(sconv_norm)),
        w_up=w_mlp_up.astype(BF16), w_dn=w_mlp_down.astype(BF16),
    )


def _run_trunk(x, states, w, prompt):
    batch, seq, _ = x.shape
    depth = w["w_in"].shape[0]
    m = batch * seq
    x2d = x.reshape(m, D_MODEL)
    tm = 512 if m % 512 == 0 else m
    if prompt:
        wb = min(WIN_MAX, seq)
    else:
        cache_k, cache_v, st_ssm, st_conv, st_sconv = states
        wb = cache_k.shape[2]
        cache_k = jnp.transpose(cache_k, (0, 1, 3, 4, 2)).reshape(depth, batch, D_A, wb)
        cache_v = jnp.transpose(cache_v, (0, 1, 3, 4, 2)).reshape(depth, batch, D_A, wb)
        h0 = st_ssm.astype(F32)
        cprev8 = jnp.pad(st_conv.astype(F32), ((0, 0), (0, 0), (8 - (CONV_W - 1), 0), (0, 0)))
        sprev8 = jnp.pad(st_sconv.astype(F32), ((0, 0), (0, 0), (8 - (SCONV_W - 1), 0), (0, 0)))
    new_k = lax.empty((depth, batch, D_A, wb), F32)
    new_v = lax.empty((depth, batch, D_A, wb), F32)
    st_out = (lax.empty((depth, batch, H_B, HEAD_DIM, N_STATE), F32),
              lax.empty((depth, batch, 8, CONV_DIM), F32),
              lax.empty((depth, batch, 8, D_C), F32))
    for l in range(depth):
        if prompt:
            proj, qkv_perm, new_k, new_v = _in_proj(x2d, w["g_mix_pre"], w["w_in"], l, tm, seq, wb,
                                                    new_k, new_v)
            oa0, oa1 = _attn_prompt(qkv_perm, w["attn_norm"], l, batch, seq)
            x2d, *st_out = _fused(x2d, oa0, oa1, proj, w["w_out_a"], w["w_out_bc"],
                                  w["g_mix_post"], w["g_mlp_pre"], w["g_mlp_post"],
                                  w["w_up"], w["w_dn"], w["ssd"], st_out, l, batch, tm, 512)
        else:
            (proj,) = _in_proj(x2d, w["g_mix_pre"], w["w_in"], l, tm)
            oa0, oa1, new_k, new_v = _attn_sample(
                proj, cache_k, cache_v, w["attn_norm"], new_k, new_v, l, batch, seq, wb,
                ATTN_SAMPLE_GROUP if batch % ATTN_SAMPLE_GROUP == 0 else 1)
            ybc, *st_out = _ssd(proj, h0, cprev8, sprev8, w["ssd"], st_out, l, batch, 1, seq,
                                SSD_GROUP if batch % SSD_GROUP == 0 else 1)
            ybc = ybc.reshape(m, D_B + D_C)
            x2d = _post(x2d, oa0, oa1, ybc, w["w_out_a"], w["w_out_bc"], w["g_mix_post"],
                        w["g_mlp_pre"], w["g_mlp_post"], w["w_up"], w["w_dn"], l, tm, 1024)
    h_all, c_all, s_all = st_out
    outs = (jnp.transpose(new_k.reshape(depth, batch, H_A, HEAD_DIM, wb), (0, 1, 4, 2, 3)),
            jnp.transpose(new_v.reshape(depth, batch, H_A, HEAD_DIM, wb), (0, 1, 4, 2, 3)),
            h_all, c_all[:, :, 8 - (CONV_W - 1):, :], s_all[:, :, 8 - (SCONV_W - 1):, :])
    return x2d.reshape(batch, seq, D_MODEL), outs


def kernel(x_prompt, x_sample, cache_attn_k, cache_attn_v, state_ssm, state_ssm_conv, state_sconv, norm_mix_pre, norm_mix_post, norm_mlp_pre, norm_mlp_post, w_in, w_out, attn_norm, ssm_conv_w, ssm_conv_b, ssm_a_log, ssm_dt_bias, ssm_d, ssm_norm, sconv_w, sconv_norm, w_mlp_up, w_mlp_down):
    w = _prep_weights(norm_mix_pre, norm_mix_post, norm_mlp_pre, norm_mlp_post, w_in, w_out,
                      attn_norm, ssm_conv_w, ssm_conv_b, ssm_a_log, ssm_dt_bias, ssm_d,
                      ssm_norm, sconv_w, sconv_norm, w_mlp_up, w_mlp_down)
    y_prompt, (p_k, p_v, p_ssm, p_conv, p_sconv) = _run_trunk(x_prompt, None, w, True)
    states = (cache_attn_k, cache_attn_v, state_ssm, state_ssm_conv, state_sconv)
    y_sample, (s_k, s_v, s_ssm, s_conv, s_sconv) = _run_trunk(x_sample, states, w, False)
    return (y_prompt, y_sample, p_k, p_v, p_ssm, p_conv, p_sconv,
            s_k, s_v, s_ssm, s_conv, s_sconv)
```

```python
import functools
import math

import jax
import jax.numpy as jnp
from jax import lax
from jax.experimental import pallas as pl
from jax.experimental.pallas import tpu as pltpu

F32 = jnp.float32
BF16 = jnp.bfloat16

HEAD_DIM = 64
D_MODEL = 1024
D_A = 256
H_A = 4
D_B = 512
H_B = 8
SSM_GROUPS = 2
N_STATE = 128
CONV_W = 4
CONV_DIM = D_B + 2 * SSM_GROUPS * N_STATE
D_C = 256
SCONV_W = 3
D_FF = 4 * D_MODEL
DILATED_CONFIGS = ((128, 1), (512, 4), (2048, 16))
WIN_MAX = 2048
EPS = 1e-6
LOG2E = 1.4426950408889634
ALIBI_SLOPES = tuple(2.0 ** (-8.0 * (h + 1) / H_A) for h in range(H_A))
ATTN_BLOCK = 128
ATTN_PERM = 16
ATTN_BLOCKS_PER_STEP = 16
ATTN_SAMPLE_GROUP = 2
SSD_CHUNK = 128
SSD_GROUP = 8
SSD_STAGES = 6

COL_XBC = 0
COL_Z = 1024
COL_Q = 1536
COL_G = 2304
COL_DT = 3072
PROJ_W = 3200

VMEM_LIMIT = 56 * 1024 * 1024


def _rms(x, g):
    return x * lax.rsqrt(jnp.mean(x * x, axis=-1, keepdims=True) + EPS) * g


def _silu(x):
    return x * (1.0 / (1.0 + jnp.exp(-x)))


def _softplus(x):
    return jnp.maximum(x, 0.0) + jnp.log1p(jnp.exp(-jnp.abs(x)))


def _in_proj_kernel(x_ref, g_ref, w_ref, *rest, permute):
    o_ref = rest[2] if permute else rest[0]
    h = _rms(x_ref[...], g_ref[...]).astype(BF16)
    for lo, hi in ((0, COL_Q), (COL_Q, COL_G), (COL_G, PROJ_W)):
        res = jnp.dot(h, w_ref[:, lo:hi], preferred_element_type=F32)
        o_ref[:, lo:hi] = res
        if permute and lo == COL_Q:
            operm_ref, pkt_ref, pvt_ref = rest[3:]
            tm = x_ref.shape[0]
            pkt_ref[...] = res[:, D_A:2 * D_A].T
            pvt_ref[...] = res[:, 2 * D_A:3 * D_A].T
            operm_ref[...] = jnp.swapaxes(
                res.reshape(tm // ATTN_PERM, ATTN_PERM, 3 * D_A), 0, 1)


def _in_proj(x2d, g, w, layer, tm, seq=None, wb=None, pkt=None, pvt=None):
    m = x2d.shape[0]
    permute = seq is not None
    in_specs = [
        pl.BlockSpec((tm, D_MODEL), lambda i: (i, 0)),
        pl.BlockSpec((None, 1, D_MODEL), lambda i: (layer, 0, 0)),
        pl.BlockSpec((None, D_MODEL, PROJ_W), lambda i: (layer, 0, 0),
                     pipeline_mode=pl.Buffered(1)),
    ]
    args = [x2d, g, w]
    out_specs = [pl.BlockSpec((tm, PROJ_W), lambda i: (i, 0))]
    out_shape = [jax.ShapeDtypeStruct((m, PROJ_W), F32)]
    scratch = []
    aliases = {}
    if permute:
        tiles = seq // tm
        rows = tm // ATTN_PERM
        first = (seq - wb) // tm
        out_specs.append(pl.BlockSpec((None, ATTN_PERM, rows, 3 * D_A),
                                      lambda i: (i // tiles, 0, i % tiles, 0)))
        out_shape.append(jax.ShapeDtypeStruct((m // seq, ATTN_PERM, seq // ATTN_PERM, 3 * D_A), F32))
        win_spec = pl.BlockSpec((None, None, D_A, tm),
                                lambda i: (layer, i // tiles, 0, jnp.maximum(i % tiles - first, 0)))
        out_specs += [win_spec, win_spec]
        out_shape += [jax.ShapeDtypeStruct(pkt.shape, F32)] * 2
        in_specs += [pl.BlockSpec(memory_space=pl.ANY)] * 2
        args += [pkt, pvt]
        aliases = {3: 2, 4: 3}
    return pl.pallas_call(
        functools.partial(_in_proj_kernel, permute=permute),
        grid=(m // tm,),
        in_specs=in_specs,
        out_specs=out_specs,
        out_shape=out_shape,
        input_output_aliases=aliases,
        scratch_shapes=scratch,
        compiler_params=pltpu.CompilerParams(
            dimension_semantics=("arbitrary",), vmem_limit_bytes=VMEM_LIMIT),
        name="in_proj",
    )(*args)


def _attn_prompt_kernel(q_ref, k_ref, v_ref, g_ref, o0_ref, o1_ref,
                        acc_ref, m_ref, l_ref, bias_ref, *, seq):
    blk = ATTN_BLOCK
    nbr = len(DILATED_CONFIGS)
    P = ATTN_PERM
    per_res = seq // P

    def local_to_strided(a, dil):
        nchunk = P // dil
        rows = blk // nchunk
        return nchunk * (a % rows) + a // rows

    shape4 = (H_A * blk, 2 * blk)
    row4 = lax.broadcasted_iota(jnp.int32, shape4, 0)
    ki = lax.broadcasted_iota(jnp.int32, shape4, 1)
    slope = jnp.zeros(shape4, F32)
    for h in range(H_A):
        slope = jnp.where(row4 // blk == h, ALIBI_SLOPES[h], slope)
    for bi, (win, dil) in enumerate(DILATED_CONFIGS):
        assert win // dil == blk and P % dil == 0
        jq = blk + local_to_strided(row4 % blk, dil)
        jk = local_to_strided(ki % blk, dil) + blk * (ki // blk)
        diff = jq - jk
        band = (diff >= 0) & (diff <= blk)
        dist = slope * (-float(dil) * LOG2E * diff.astype(F32))
        bias_ref[bi] = jnp.where(band, dist, -jnp.inf)
        bias_ref[nbr + bi] = jnp.where(band & (ki >= blk), dist, -jnp.inf)

    lane_head = lax.broadcasted_iota(jnp.int32, (blk, D_A), 1) // HEAD_DIM
    lane_lo = lax.broadcasted_iota(jnp.int32, (blk, 128), 1) < HEAD_DIM

    def per_head(col):
        c = [col[h * blk:(h + 1) * blk, :] for h in range(H_A)]
        return jnp.concatenate([jnp.where(lane_lo, c[0], c[1]), jnp.where(lane_lo, c[2], c[3])],
                               axis=1)

    def blocks(r, u, bi, dil):
        nchunk = P // dil
        rows = blk // nchunk
        nb = min(ATTN_BLOCKS_PER_STEP, per_res // rows)
        starts = [pl.multiple_of(jnp.maximum(nb * u - 1 + j, 0) * rows, rows)
                  for j in range(nb + 1)]

        def load(ref, st):
            return jnp.concatenate(
                [ref[dil * c + r, pl.ds(st, rows), :] for c in range(nchunk)], axis=0)

        def store(ref, st, val):
            for c in range(nchunk):
                ref[dil * c + r, pl.ds(st, rows), :] = val[c * rows:(c + 1) * rows, :]

        kb = [load(k_ref, st).astype(BF16) for st in starts]
        vb = [load(v_ref, st).astype(BF16) for st in starts]
        results = []
        for j in range(nb):
            q = (load(q_ref, starts[1 + j]) * (HEAD_DIM ** -0.5 * LOG2E)).astype(BF16)
            q4 = jnp.concatenate(
                [jnp.where(lane_head == h, q, jnp.zeros_like(q)) for h in range(H_A)], axis=0)
            kk = jnp.concatenate([kb[j], kb[j + 1]], axis=0)
            vv = jnp.concatenate([vb[j], vb[j + 1]], axis=0)
            s = lax.dot_general(q4, kk, (((1,), (1,)), ((), ())), preferred_element_type=F32)
            if j == 0:
                s = s + bias_ref[jnp.where(u == 0, nbr + bi, bi)]
            else:
                s = s + bias_ref[bi]
            m4 = jnp.max(s, axis=-1, keepdims=True)
            p = jnp.exp2(s - m4)
            l4 = jnp.sum(p, axis=-1, keepdims=True)
            pb = p.astype(BF16)
            o = [jnp.dot(pb[h * blk:(h + 1) * blk, :],
                         vv[:, (h // 2) * 128:(h // 2 + 1) * 128],
                         preferred_element_type=F32) for h in range(H_A)]
            acc_b = jnp.concatenate([jnp.where(lane_lo, o[0], o[1]),
                                     jnp.where(lane_lo, o[2], o[3])], axis=1)
            m_b = per_head(m4)
            l_b = per_head(l4)
            if bi > 0:
                st = starts[1 + j]
                m_old = load(m_ref, st)
                m_new = jnp.maximum(m_old, m_b)
                a_old = jnp.exp2(m_old - m_new)
                a_b = jnp.exp2(m_b - m_new)
                acc_b = load(acc_ref, st) * a_old + acc_b * a_b
                l_b = load(l_ref, st) * a_old + l_b * a_b
                m_b = m_new
            results.append((acc_b, m_b, l_b))
        for j in range(nb):
            store(acc_ref, starts[1 + j], results[j][0])
            store(m_ref, starts[1 + j], results[j][1])
            store(l_ref, starts[1 + j], results[j][2])

    for bi, (win, dil) in enumerate(DILATED_CONFIGS):
        nblk = per_res // (blk // (P // dil))
        nb = min(ATTN_BLOCKS_PER_STEP, nblk)
        nstep = nblk // nb
        nres = ATTN_BLOCKS_PER_STEP // nb
        assert dil % nres == 0

        def body(it, carry, bi=bi, dil=dil, nstep=nstep, nres=nres):
            rg = it // nstep
            u = it - rg * nstep
            for rr in range(nres):
                blocks(rg * nres + rr, u, bi, dil)
            return carry

        lax.fori_loop(0, (dil // nres) * nstep, body, 0)

    def finish(r, carry):
        o = _rms(acc_ref[r] / l_ref[r], g_ref[...])
        o0_ref[pl.ds(r, per_res, stride=P), :] = o[:, 0:128]
        o1_ref[pl.ds(r, per_res, stride=P), :] = o[:, 128:256]
        return carry

    lax.fori_loop(0, P, finish, 0)


def _attn_prompt(qkv_perm, g, layer, batch, seq):
    per_res = seq // ATTN_PERM
    qkv_specs = [pl.BlockSpec((None, ATTN_PERM, per_res, D_A),
                              functools.partial(lambda b, j: (b, 0, 0, j), j=j)) for j in range(3)]
    half_spec = pl.BlockSpec((seq, 128), lambda b: (b, 0))
    return pl.pallas_call(
        functools.partial(_attn_prompt_kernel, seq=seq),
        grid=(batch,),
        in_specs=qkv_specs + [pl.BlockSpec((None, 1, D_A), lambda b: (layer, 0, 0))],
        out_specs=[half_spec, half_spec],
        out_shape=[jax.ShapeDtypeStruct((batch * seq, 128), F32)] * 2,
        scratch_shapes=[pltpu.VMEM((ATTN_PERM, per_res, D_A), F32)] * 3
        + [pltpu.VMEM((2 * len(DILATED_CONFIGS), H_A * ATTN_BLOCK, 2 * ATTN_BLOCK), F32)],
        compiler_params=pltpu.CompilerParams(
            dimension_semantics=("parallel",), vmem_limit_bytes=VMEM_LIMIT),
        name="attn_prompt",
    )(qkv_perm, qkv_perm, qkv_perm, g)


def _attn_sample_kernel(q_ref, k_ref, v_ref, ckt_ref, cvt_ref, g_ref, nk_in_ref, nv_in_ref,
                        o0_ref, o1_ref, nkt_ref, nvt_ref, kpad_ref, vpad_ref, *, wb, s_len, group):
    pad = 128
    rows = H_A * s_len
    ncol = wb + pad
    row_i = lax.broadcasted_iota(jnp.int32, (rows, ncol), 0)
    col_i = lax.broadcasted_iota(jnp.int32, (rows, ncol), 1)
    s_i = row_i % s_len
    pos = jnp.where(col_i < wb, col_i, col_i + s_len - pad)
    d = wb + s_i - pos
    real = (col_i < wb) | (col_i >= ncol - s_len)
    mult = jnp.zeros((rows, ncol), F32)
    for win, dil in DILATED_CONFIGS:
        hit = real & (d >= 0) & (d <= win) & ((d % dil) == 0)
        mult = mult + jnp.where(hit, 1.0, 0.0)
    slope = jnp.zeros((rows, ncol), F32)
    for h in range(H_A):
        slope = jnp.where(row_i // s_len == h, ALIBI_SLOPES[h], slope)
    bias = jnp.where(mult > 0.0, -slope * d.astype(F32), -jnp.inf)
    tail = lax.broadcasted_iota(jnp.int32, (D_A, pad), 1) >= pad - s_len
    lane_head = lax.broadcasted_iota(jnp.int32, (s_len, D_A), 1) // HEAD_DIM
    nt = (((1,), (1,)), ((), ()))

    for i in range(group):
        tok = slice(i * s_len, (i + 1) * s_len)
        k_new = k_ref[tok, :]
        v_new = v_ref[tok, :]
        kpad_ref[i] = jnp.zeros(kpad_ref.shape[1:], F32)
        vpad_ref[i] = jnp.zeros(vpad_ref.shape[1:], F32)
        kpad_ref[i, pad - s_len:pad, :] = k_new
        vpad_ref[i, pad - s_len:pad, :] = v_new
        kpad = kpad_ref[i]
        vpad = vpad_ref[i]
        kpad_t = kpad.T
        vpad_t = vpad.T

        for src_ref, dst_ref, new_t in ((ckt_ref, nkt_ref, kpad_t), (cvt_ref, nvt_ref, vpad_t)):
            rolled = pltpu.roll(src_ref[i], wb - s_len, axis=1)
            dst_ref[i, :, 0:wb - pad] = rolled[:, 0:wb - pad]
            dst_ref[i, :, wb - pad:wb] = jnp.where(tail, new_t, rolled[:, wb - pad:wb])

        q = q_ref[tok, :] * (HEAD_DIM ** -0.5)
        qh = jnp.concatenate([jnp.where(lane_head == h, q, 0.0) for h in range(H_A)],
                             axis=0).astype(BF16)
        s1 = jnp.dot(qh, ckt_ref[i].astype(BF16), preferred_element_type=F32)
        s2 = jnp.dot(qh, kpad_t.astype(BF16), preferred_element_type=F32)
        s = jnp.concatenate([s1, s2], axis=1) + bias
        m = jnp.max(s, axis=-1, keepdims=True)
        p = jnp.exp(s - m) * mult
        l = jnp.sum(p, axis=-1, keepdims=True)
        pb = p.astype(BF16)
        o = lax.dot_general(pb[:, 0:wb], cvt_ref[i].astype(BF16), nt, preferred_element_type=F32)
        o = o + jnp.dot(pb[:, wb:ncol], vpad.astype(BF16), preferred_element_type=F32)
        o = o / l
        out = jnp.zeros((s_len, D_A), F32)
        for h in range(H_A):
            out = jnp.where(lane_head == h, o[h * s_len:(h + 1) * s_len, :], out)
        out = _rms(out, g_ref[...])
        o0_ref[tok, :] = out[:, 0:128]
        o1_ref[tok, :] = out[:, 128:256]


def _attn_sample(proj, cache_k, cache_v, g, new_k, new_v, layer, batch, s_len, wb, group):
    depth = cache_k.shape[0]
    qb = COL_Q // D_A
    cache_spec = pl.BlockSpec((None, group, D_A, wb), lambda b: (layer, b, 0, 0))
    rows = group * s_len
    return pl.pallas_call(
        functools.partial(_attn_sample_kernel, wb=wb, s_len=s_len, group=group),
        grid=(batch // group,),
        in_specs=[
            pl.BlockSpec((rows, D_A), lambda b: (b, qb)),
            pl.BlockSpec((rows, D_A), lambda b: (b, qb + 1)),
            pl.BlockSpec((rows, D_A), lambda b: (b, qb + 2)),
            cache_spec, cache_spec,
            pl.BlockSpec((None, 1, D_A), lambda b: (layer, 0, 0)),
            pl.BlockSpec(memory_space=pl.ANY), pl.BlockSpec(memory_space=pl.ANY),
        ],
        out_specs=[pl.BlockSpec((rows, 128), lambda b: (b, 0))] * 2 + [cache_spec, cache_spec],
        out_shape=[jax.ShapeDtypeStruct((batch * s_len, 128), F32)] * 2 + [
                   jax.ShapeDtypeStruct((depth, batch, D_A, wb), F32)] * 2,
        input_output_aliases={6: 2, 7: 3},
        scratch_shapes=[pltpu.VMEM((group, 128, D_A), F32)] * 2,
        compiler_params=pltpu.CompilerParams(
            dimension_semantics=("parallel",), vmem_limit_bytes=VMEM_LIMIT),
        name="attn_sample",
    )(proj, proj, proj, cache_k, cache_v, g, new_k, new_v)


def _ssd_kernel(xbc_ref, z_ref, gate_ref, dtr_ref, h0_ref, cprev_ref, sprev_ref,
                cw_ref, cb_ref, alog_ref, dtb_ref, dsk_ref, sn_ref, scw_ref, scn_ref,
                hall_ref, call_ref, sall_ref,
                y_ref, hout_ref, cnew_ref, snew_ref,
                ext_ref, ext2_ref, h_ref, *, chunk, group):
    c = pl.program_id(1)
    nc = pl.num_programs(1)

    @pl.when(c == 0)
    def _init():
        h_ref[...] = h0_ref[...]
        ext_ref[:, 0:8, :] = cprev_ref[...]
        ext2_ref[:, 0:8, :] = sprev_ref[...]

    for i in range(group):
        _ssd_one(xbc_ref.at[i], z_ref.at[i], gate_ref.at[i], dtr_ref.at[i],
                 cw_ref, cb_ref, alog_ref, dtb_ref, dsk_ref, sn_ref, scw_ref, scn_ref,
                 y_ref.at[i], ext_ref.at[i], ext2_ref.at[i], h_ref.at[i], chunk=chunk)

    @pl.when(c == nc - 1)
    def _final():
        hout_ref[...] = h_ref[...]
        cnew_ref[...] = ext_ref[:, chunk:chunk + 8, :]
        snew_ref[...] = ext2_ref[:, chunk:chunk + 8, :]

    ext_ref[:, 0:8, :] = ext_ref[:, chunk:chunk + 8, :]
    ext2_ref[:, 0:8, :] = ext2_ref[:, chunk:chunk + 8, :]


def _ssd_one(*args, **kwargs):
    for _ in _ssd_stages(*args, **kwargs):
        pass


def _ssd_stages(xbc_ref, z_ref, gate_ref, dtr_ref,
                cw_ref, cb_ref, alog_ref, dtb_ref, dsk_ref, sn_ref, scw_ref, scn_ref,
                y_ref, ext_ref, ext2_ref, h_ref, *, chunk):
    C = chunk
    heads_per_group = H_B // SSM_GROUPS
    gw = heads_per_group * HEAD_DIM

    ext_ref[8:8 + C, :] = xbc_ref[...]
    xc_parts = []
    half = CONV_DIM // 2
    for lo in (0, half):
        conv = xbc_ref[:, lo:lo + half] * cw_ref[CONV_W - 1:CONV_W, lo:lo + half]
        for k in range(CONV_W - 1):
            off = 8 - (CONV_W - 1) + k
            conv = conv + ext_ref[off:off + C, lo:lo + half] * cw_ref[k:k + 1, lo:lo + half]
        xc_parts.append(_silu(conv + cb_ref[:, lo:lo + half]))
        yield
    x_s = xc_parts[0]
    bm = xc_parts[1][:, 0:SSM_GROUPS * N_STATE]
    cm = xc_parts[1][:, SSM_GROUPS * N_STATE:2 * SSM_GROUPS * N_STATE]

    gates = gate_ref[...]
    b_gate = gates[:, 0:D_C]
    prod = gates[:, D_C:2 * D_C] * gates[:, 2 * D_C:3 * D_C]
    ext2_ref[8:8 + C, :] = prod
    sconv = prod * scw_ref[SCONV_W - 1:SCONV_W, :]
    for k in range(SCONV_W - 1):
        off = 8 - (SCONV_W - 1) + k
        sconv = sconv + ext2_ref[off:off + C, :] * scw_ref[k:k + 1, :]
    y_ref[:, D_B:D_B + D_C] = _rms(b_gate * sconv, scn_ref[...]).astype(y_ref.dtype)

    row = lax.broadcasted_iota(jnp.int32, (C, C), 0)
    col = lax.broadcasted_iota(jnp.int32, (C, C), 1)
    causal = row >= col
    dt8 = _softplus((dtr_ref[...] + dtb_ref[...]).T[0:H_B, :])
    a8 = jnp.broadcast_to(-jnp.exp(alog_ref[...]), (128, 128)).T[0:H_B, 0:C]
    da = dt8 * a8
    upper = jnp.where(row <= col, 1.0, 0.0).astype(BF16)
    da_hi = da.astype(BF16)
    rem = da - da_hi.astype(F32)
    da_mid = rem.astype(BF16)
    da_lo = (rem - da_mid.astype(F32)).astype(BF16)
    acs8 = (jnp.dot(da_hi, upper, preferred_element_type=F32)
            + jnp.dot(da_mid, upper, preferred_element_type=F32)
            + jnp.dot(da_lo, upper, preferred_element_type=F32))
    last8 = acs8[:, C - 1:C]
    cdec8 = jnp.exp(last8)
    stacked = jnp.concatenate(
        [acs8, jnp.exp(acs8), jnp.exp(last8 - acs8) * dt8, jnp.zeros((128 - 3 * H_B, C), F32)],
        axis=0)
    cols = stacked.T
    lane_lo = lax.broadcasted_iota(jnp.int32, (C, 2 * HEAD_DIM), 1) < HEAD_DIM

    def expand(base):
        return jnp.concatenate(
            [jnp.where(lane_lo, cols[:, base + 2 * p:base + 2 * p + 1],
                       cols[:, base + 2 * p + 1:base + 2 * p + 2]) for p in range(H_B // 2)],
            axis=1)

    e_acs_x = expand(H_B)
    w_end_x = expand(2 * H_B)
    yield

    nt = (((1,), (1,)), ((), ()))
    tn = (((0,), (0,)), ((), ()))
    x_bf = x_s.astype(BF16)
    xw = (x_s * w_end_x)
    y_parts = []
    for g in range(SSM_GROUPS):
        bm_g = bm[:, g * N_STATE:(g + 1) * N_STATE].astype(BF16)
        cm_g = cm[:, g * N_STATE:(g + 1) * N_STATE].astype(BF16)
        scores = lax.dot_general(cm_g, bm_g, nt, preferred_element_type=F32)
        h_g = h_ref[g * heads_per_group:(g + 1) * heads_per_group].reshape(gw, N_STATE)
        y_off = lax.dot_general(cm_g, h_g.astype(BF16), nt, preferred_element_type=F32)
        y_off = y_off * e_acs_x[:, g * gw:(g + 1) * gw]
        diag = []
        for pair in range(heads_per_group // 2):
            mats = []
            for hh in range(2):
                h = g * heads_per_group + 2 * pair + hh
                seg = cols[:, h:h + 1] - acs8[h:h + 1, :]
                decay = jnp.exp(jnp.where(causal, seg, -jnp.inf))
                mats.append((scores * decay * dt8[h:h + 1, :]).astype(BF16))
            lo = (g * heads_per_group + 2 * pair) * HEAD_DIM
            x_pair = x_bf[:, lo:lo + 2 * HEAD_DIM]
            zero = jnp.zeros_like(x_pair)
            x_blk = jnp.concatenate([jnp.where(lane_lo, x_pair, zero),
                                     jnp.where(lane_lo, zero, x_pair)], axis=0)
            diag.append(jnp.dot(jnp.concatenate(mats, axis=1), x_blk,
                                preferred_element_type=F32))
        y_parts.append(jnp.concatenate(diag, axis=1) + y_off)
        st = lax.dot_general(xw[:, g * gw:(g + 1) * gw].astype(BF16), bm_g, tn,
                             preferred_element_type=F32)
        for hh in range(heads_per_group):
            h = g * heads_per_group + hh
            h_ref[h] = h_ref[h] * cdec8[h:h + 1, 0:1] + st[hh * HEAD_DIM:(hh + 1) * HEAD_DIM, :]
        yield
    y = jnp.concatenate(y_parts, axis=1) + dsk_ref[...] * x_s
    y_ref[:, 0:D_B] = _rms(y * _silu(z_ref[...]), sn_ref[...]).astype(y_ref.dtype)
    yield


def _ssd(proj, h0, cprev8, sprev8, wts, outs, layer, batch, nc, chunk, group):
    (cw, cb, alog, dtb, dsk, sn, scw, scn) = wts
    proj3 = proj.reshape(batch, nc * chunk, PROJ_W)

    def tok(width, blk_idx):
        return pl.BlockSpec((group, chunk, width), lambda b, c: (b, c, blk_idx))

    def per_layer(shape):
        return pl.BlockSpec((None,) + shape, lambda b, c: (layer,) + (0,) * len(shape))

    state_spec = pl.BlockSpec((group, H_B, HEAD_DIM, N_STATE), lambda b, c: (b, 0, 0, 0))
    conv_spec = pl.BlockSpec((group, 8, CONV_DIM), lambda b, c: (b, 0, 0))
    sconv_spec = pl.BlockSpec((group, 8, D_C), lambda b, c: (b, 0, 0))

    def stacked(spec):
        return pl.BlockSpec((None,) + tuple(spec.block_shape),
                            lambda b, c: (layer,) + tuple(spec.index_map(b, c)))

    return pl.pallas_call(
        functools.partial(_ssd_kernel, chunk=chunk, group=group),
        grid=(batch // group, nc),
        in_specs=[
            tok(CONV_DIM, COL_XBC // CONV_DIM),
            tok(D_B, COL_Z // D_B),
            tok(3 * D_C, COL_G // (3 * D_C)),
            tok(128, COL_DT // 128),
            stacked(state_spec), stacked(conv_spec), stacked(sconv_spec),
            per_layer((CONV_W, CONV_DIM)), per_layer((1, CONV_DIM)),
            per_layer((1, 128)), per_layer((1, 128)), per_layer((1, D_B)),
            per_layer((1, D_B)), per_layer((SCONV_W, D_C)), per_layer((1, D_C)),
        ] + [pl.BlockSpec(memory_space=pl.ANY)] * 3,
        out_specs=[
            pl.BlockSpec((group, chunk, D_B + D_C), lambda b, c: (b, c, 0)),
            stacked(state_spec), stacked(conv_spec), stacked(sconv_spec),
        ],
        out_shape=[jax.ShapeDtypeStruct((batch, nc * chunk, D_B + D_C), BF16)]
        + [jax.ShapeDtypeStruct(o.shape, F32) for o in outs],
        input_output_aliases={15: 1, 16: 2, 17: 3},
        scratch_shapes=[
            pltpu.VMEM((group, chunk + 8, CONV_DIM), F32),
            pltpu.VMEM((group, chunk + 8, D_C), F32),
            pltpu.VMEM((group, H_B, HEAD_DIM, N_STATE), F32),
        ],
        compiler_params=pltpu.CompilerParams(
            dimension_semantics=("parallel", "arbitrary"), vmem_limit_bytes=VMEM_LIMIT),
        name="ssd_sconv",
    )(proj3, proj3, proj3, proj3, h0, cprev8, sprev8, cw, cb, alog, dtb, dsk, sn, scw, scn, *outs)


def _post_kernel(x_ref, oa0_ref, oa1_ref, ybc_ref, woa_ref, wobc_ref, gpost_ref, gpre_ref,
                 gmlp_ref, wup_ref, wdn_ref, o_ref, acc_ref, *, tf):
    oa = jnp.concatenate([oa0_ref[...], oa1_ref[...]], axis=1).astype(BF16)
    mix = jnp.dot(oa, woa_ref[...], preferred_element_type=F32)
    mix = mix + jnp.dot(ybc_ref[...], wobc_ref[...], preferred_element_type=F32)
    x1 = x_ref[...] + _rms(mix, gpost_ref[...])
    h = _rms(x1, gpre_ref[...]).astype(BF16)
    for j in range(D_FF // tf):
        u = jnp.dot(h, wup_ref[:, j * tf:(j + 1) * tf], preferred_element_type=F32)
        u = jnp.square(jnp.maximum(u, 0.0)).astype(BF16)
        f = jnp.dot(u, wdn_ref[j * tf:(j + 1) * tf, :], preferred_element_type=F32)
        if j == 0:
            acc_ref[...] = f
        else:
            acc_ref[...] += f
    o_ref[...] = x1 + _rms(acc_ref[...], gmlp_ref[...])


def _post(x2d, oa0, oa1, ybc, w_out_a, w_out_bc, gpost, gpre, gmlp, w_up, w_dn, layer, tm, tf):
    m = x2d.shape[0]

    def resident(shape):
        return pl.BlockSpec((None,) + shape, lambda i: (layer, 0, 0),
                            pipeline_mode=pl.Buffered(1))

    return pl.pallas_call(
        functools.partial(_post_kernel, tf=tf),
        grid=(m // tm,),
        in_specs=[
            pl.BlockSpec((tm, D_MODEL), lambda i: (i, 0)),
            pl.BlockSpec((tm, 128), lambda i: (i, 0)),
            pl.BlockSpec((tm, 128), lambda i: (i, 0)),
            pl.BlockSpec((tm, D_B + D_C), lambda i: (i, 0)),
            resident((D_A, D_MODEL)), resident((D_B + D_C, D_MODEL)),
            resident((1, D_MODEL)), resident((1, D_MODEL)), resident((1, D_MODEL)),
            resident((D_MODEL, D_FF)), resident((D_FF, D_MODEL)),
        ],
        out_specs=pl.BlockSpec((tm, D_MODEL), lambda i: (i, 0)),
        out_shape=jax.ShapeDtypeStruct((m, D_MODEL), F32),
        scratch_shapes=[pltpu.VMEM((tm, D_MODEL), F32)],
        compiler_params=pltpu.CompilerParams(
            dimension_semantics=("parallel",), vmem_limit_bytes=VMEM_LIMIT),
        name="post",
    )(x2d, oa0, oa1, ybc, w_out_a, w_out_bc, gpost, gpre, gmlp, w_up, w_dn)


def _fused_kernel(x_ref, oa0_ref, oa1_ref, woa_ref, wobc_ref, gpost_ref, gpre_ref, gmlp_ref,
                  wup_ref, wdn_ref,
                  xbc_ref, z_ref, gate_ref, dtr_ref,
                  cw_ref, cb_ref, alog_ref, dtb_ref, dsk_ref, sn_ref, scw_ref, scn_ref,
                  hall_ref, call_ref, sall_ref,
                  o_ref, hout_ref, cnew_ref, snew_ref,
                  acc_ref, ybc_ref, ext_ref, ext2_ref, h_ref, *, tf, tm, tiles_per_seq, ntiles):
    i = pl.program_id(0)
    slot = i % 2
    nchunks = tm // SSD_CHUNK

    @pl.when(i == 0)
    def _first():
        ybc_ref[...] = jnp.zeros(ybc_ref.shape, ybc_ref.dtype)

    @pl.when(i % tiles_per_seq == 0)
    def _new_sequence():
        h_ref[...] = jnp.zeros(h_ref.shape, F32)
        ext_ref[0:8, :] = jnp.zeros((8, CONV_DIM), F32)
        ext2_ref[0:8, :] = jnp.zeros((8, D_C), F32)

    def ssd_chunk(c):
        rows = slice(c * SSD_CHUNK, (c + 1) * SSD_CHUNK)
        yield from _ssd_stages(
            xbc_ref.at[rows], z_ref.at[rows], gate_ref.at[rows], dtr_ref.at[rows],
            cw_ref, cb_ref, alog_ref, dtb_ref, dsk_ref, sn_ref, scw_ref, scn_ref,
            ybc_ref.at[slot, rows], ext_ref, ext2_ref, h_ref, chunk=SSD_CHUNK)
        ext_ref[0:8, :] = ext_ref[SSD_CHUNK:SSD_CHUNK + 8, :]
        ext2_ref[0:8, :] = ext2_ref[SSD_CHUNK:SSD_CHUNK + 8, :]

    def ssd_tile():
        for c in range(nchunks):
            yield from ssd_chunk(c)

    stages = ssd_tile()
    nff = D_FF // tf
    per_ff = -(-(nchunks * SSD_STAGES) // (nff + 1))

    def advance():
        for _ in range(per_ff):
            next(stages, None)

    oa = jnp.concatenate([oa0_ref[...], oa1_ref[...]], axis=1).astype(BF16)
    mix = jnp.dot(oa, woa_ref[...], preferred_element_type=F32)
    mix = mix + jnp.dot(ybc_ref[1 - slot], wobc_ref[...], preferred_element_type=F32)
    x1 = x_ref[...] + _rms(mix, gpost_ref[...])
    h = _rms(x1, gpre_ref[...]).astype(BF16)
    advance()
    for j in range(nff):
        u = jnp.dot(h, wup_ref[:, j * tf:(j + 1) * tf], preferred_element_type=F32)
        u = jnp.square(jnp.maximum(u, 0.0)).astype(BF16)
        f = jnp.dot(u, wdn_ref[j * tf:(j + 1) * tf, :], preferred_element_type=F32)
        if j == 0:
            acc_ref[...] = f
        else:
            acc_ref[...] += f
        advance()
    for _ in stages:
        pass
    o_ref[...] = x1 + _rms(acc_ref[...], gmlp_ref[...])

    @pl.when((i % tiles_per_seq == tiles_per_seq - 1) & (i < ntiles))
    def _sequence_done():
        hout_ref[...] = h_ref[...]
        cnew_ref[...] = ext_ref[0:8, :]
        snew_ref[...] = ext2_ref[0:8, :]


def _fused(x2d, oa0, oa1, proj, w_out_a, w_out_bc, gpost, gpre, gmlp, w_up, w_dn, wts, outs,
           layer, batch, tm, tf):
    m = x2d.shape[0]
    (cw, cb, alog, dtb, dsk, sn, scw, scn) = wts
    ntiles = m // tm
    tps = ntiles // batch

    def resident(shape):
        return pl.BlockSpec((None,) + shape, lambda i: (layer,) + (0,) * len(shape),
                            pipeline_mode=pl.Buffered(1))

    def cur(width, blk_idx):
        return pl.BlockSpec((tm, width), lambda i: (jnp.minimum(i, ntiles - 1), blk_idx))

    def prev(width):
        return pl.BlockSpec((tm, width), lambda i: (jnp.maximum(i - 1, 0), 0))

    def per_seq(shape):
        return pl.BlockSpec((None, None) + shape,
                            lambda i: (layer, jnp.minimum(i, ntiles - 1) // tps) + (0,) * len(shape))

    return pl.pallas_call(
        functools.partial(_fused_kernel, tf=tf, tm=tm, tiles_per_seq=tps, ntiles=ntiles),
        grid=(ntiles + 1,),
        in_specs=[
            prev(D_MODEL), prev(128), prev(128),
            resident((D_A, D_MODEL)), resident((D_B + D_C, D_MODEL)),
            resident((1, D_MODEL)), resident((1, D_MODEL)), resident((1, D_MODEL)),
            resident((D_MODEL, D_FF)), resident((D_FF, D_MODEL)),
            cur(CONV_DIM, COL_XBC // CONV_DIM), cur(D_B, COL_Z // D_B),
            cur(3 * D_C, COL_G // (3 * D_C)), cur(128, COL_DT // 128),
            resident((CONV_W, CONV_DIM)), resident((1, CONV_DIM)),
            resident((1, 128)), resident((1, 128)), resident((1, D_B)),
            resident((1, D_B)), resident((SCONV_W, D_C)), resident((1, D_C)),
        ] + [pl.BlockSpec(memory_space=pl.ANY)] * 3,
        out_specs=[prev(D_MODEL), per_seq((H_B, HEAD_DIM, N_STATE)), per_seq((8, CONV_DIM)),
                   per_seq((8, D_C))],
        out_shape=[jax.ShapeDtypeStruct((m, D_MODEL), F32)]
        + [jax.ShapeDtypeStruct(o.shape, F32) for o in outs],
        input_output_aliases={22: 1, 23: 2, 24: 3},
        scratch_shapes=[pltpu.VMEM((tm, D_MODEL), F32),
                        pltpu.VMEM((2, tm, D_B + D_C), BF16),
                        pltpu.VMEM((SSD_CHUNK + 8, CONV_DIM), F32),
                        pltpu.VMEM((SSD_CHUNK + 8, D_C), F32),
                        pltpu.VMEM((H_B, HEAD_DIM, N_STATE), F32)],
        compiler_params=pltpu.CompilerParams(
            dimension_semantics=("arbitrary",), vmem_limit_bytes=VMEM_LIMIT),
        name="ssd_post",
    )(x2d, oa0, oa1, w_out_a, w_out_bc, gpost, gpre, gmlp, w_up, w_dn,
      proj, proj, proj, proj, cw, cb, alog, dtb, dsk, sn, scw, scn, *outs)


def _pack_w_in_kernel(w_ref, tail_ref, o_ref, *, cols):
    for name, dst in (("xbc", COL_XBC), ("z", COL_Z), ("q", COL_Q), ("k", COL_Q + D_A),
                      ("v", COL_Q + 2 * D_A)):
        lo, hi = cols[name]
        o_ref[:, dst:dst + hi - lo] = w_ref[:, lo:hi].astype(BF16)
    o_ref[:, COL_G:PROJ_W] = tail_ref[...].astype(BF16)


def _pack_w_in(w_in, w_tail, cols):
    depth, k, n = w_in.shape
    tk = 256
    return pl.pallas_call(
        functools.partial(_pack_w_in_kernel, cols=cols),
        grid=(depth, k // tk),
        in_specs=[pl.BlockSpec((None, tk, n), lambda l, i: (l, i, 0)),
                  pl.BlockSpec((None, tk, PROJ_W - COL_G), lambda l, i: (l, i, 0))],
        out_specs=pl.BlockSpec((None, tk, PROJ_W), lambda l, i: (l, i, 0)),
        out_shape=jax.ShapeDtypeStruct((depth, k, PROJ_W), BF16),
        compiler_params=pltpu.CompilerParams(
            dimension_semantics=("parallel", "parallel"), vmem_limit_bytes=VMEM_LIMIT),
        name="pack_w_in",
    )(w_in, w_tail)


def _prep_weights(norm_mix_pre, norm_mix_post, norm_mlp_pre, norm_mlp_post, w_in, w_out,
                  attn_norm, ssm_conv_w, ssm_conv_b, ssm_a_log, ssm_dt_bias, ssm_d, ssm_norm,
                  sconv_w, sconv_norm, w_mlp_up, w_mlp_down):
    depth = w_in.shape[0]
    o = 0
    cols = {}
    for name, width in (("q", D_A), ("k", D_A), ("v", D_A), ("z", D_B), ("xbc", CONV_DIM),
                        ("dt", H_B), ("b", D_C), ("c", D_C), ("u", D_C)):
        cols[name] = (o, o + width)
        o += width

    def cs(name):
        lo, hi = cols[name]
        return w_in[:, :, lo:hi]

    pad_dt = jnp.zeros((depth, D_MODEL, 128 - H_B), w_in.dtype)
    w_tail = jnp.concatenate([cs("b"), cs("c"), cs("u"), cs("dt"), pad_dt], axis=2)
    w_in_r = _pack_w_in(w_in, w_tail, cols)

    def lane_pad(v):
        return jnp.pad(v.astype(F32), ((0, 0), (0, 128 - H_B)))[:, None, :]

    def row(v):
        return v.astype(F32)[:, None, :]

    return dict(
        g_mix_pre=row(norm_mix_pre), g_mix_post=row(norm_mix_post),
        g_mlp_pre=row(norm_mlp_pre), g_mlp_post=row(norm_mlp_post),
        w_in=w_in_r,
        w_out_a=w_out[:, 0:D_A, :].astype(BF16),
        w_out_bc=w_out[:, D_A:, :].astype(BF16),
        attn_norm=row(attn_norm),
        ssd=(ssm_conv_w.astype(F32), row(ssm_conv_b), lane_pad(ssm_a_log),
             lane_pad(ssm_dt_bias), row(jnp.repeat(ssm_d, HEAD_DIM, axis=1)),
             row(ssm_norm), sconv_w.astype(F32), row(sconv_norm)),
        w_up=w_mlp_up.astype(BF16), w_dn=w_mlp_down.astype(BF16),
    )


def _run_trunk(x, states, w, prompt):
    batch, seq, _ = x.shape
    depth = w["w_in"].shape[0]
    m = batch * seq
    x2d = x.reshape(m, D_MODEL)
    tm = 512 if m % 512 == 0 else m
    if prompt:
        wb = min(WIN_MAX, seq)
    else:
        cache_k, cache_v, st_ssm, st_conv, st_sconv = states
        wb = cache_k.shape[2]
        cache_k = jnp.transpose(cache_k, (0, 1, 3, 4, 2)).reshape(depth, batch, D_A, wb)
        cache_v = jnp.transpose(cache_v, (0, 1, 3, 4, 2)).reshape(depth, batch, D_A, wb)
        h0 = st_ssm.astype(F32)
        cprev8 = jnp.pad(st_conv.astype(F32), ((0, 0), (0, 0), (8 - (CONV_W - 1), 0), (0, 0)))
        sprev8 = jnp.pad(st_sconv.astype(F32), ((0, 0), (0, 0), (8 - (SCONV_W - 1), 0), (0, 0)))
    new_k = lax.empty((depth, batch, D_A, wb), F32)
    new_v = lax.empty((depth, batch, D_A, wb), F32)
    st_out = (lax.empty((depth, batch, H_B, HEAD_DIM, N_STATE), F32),
              lax.empty((depth, batch, 8, CONV_DIM), F32),
              lax.empty((depth, batch, 8, D_C), F32))
    for l in range(depth):
        if prompt:
            proj, qkv_perm, new_k, new_v = _in_proj(x2d, w["g_mix_pre"], w["w_in"], l, tm, seq, wb,
                                                    new_k, new_v)
            oa0, oa1 = _attn_prompt(qkv_perm, w["attn_norm"], l, batch, seq)
            x2d, *st_out = _fused(x2d, oa0, oa1, proj, w["w_out_a"], w["w_out_bc"],
                                  w["g_mix_post"], w["g_mlp_pre"], w["g_mlp_post"],
                                  w["w_up"], w["w_dn"], w["ssd"], st_out, l, batch, tm, 512)
        else:
            (proj,) = _in_proj(x2d, w["g_mix_pre"], w["w_in"], l, tm)
            oa0, oa1, new_k, new_v = _attn_sample(
                proj, cache_k, cache_v, w["attn_norm"], new_k, new_v, l, batch, seq, wb,
                ATTN_SAMPLE_GROUP if batch % ATTN_SAMPLE_GROUP == 0 else 1)
            ybc, *st_out = _ssd(proj, h0, cprev8, sprev8, w["ssd"], st_out, l, batch, 1, seq,
                                SSD_GROUP if batch % SSD_GROUP == 0 else 1)
            ybc = ybc.reshape(m, D_B + D_C)
            x2d = _post(x2d, oa0, oa1, ybc, w["w_out_a"], w["w_out_bc"], w["g_mix_post"],
                        w["g_mlp_pre"], w["g_mlp_post"], w["w_up"], w["w_dn"], l, tm, 1024)
    h_all, c_all, s_all = st_out
    outs = (jnp.transpose(new_k.reshape(depth, batch, H_A, HEAD_DIM, wb), (0, 1, 4, 2, 3)),
            jnp.transpose(new_v.reshape(depth, batch, H_A, HEAD_DIM, wb), (0, 1, 4, 2, 3)),
            h_all, c_all[:, :, 8 - (CONV_W - 1):, :], s_all[:, :, 8 - (SCONV_W - 1):, :])
    return x2d.reshape(batch, seq, D_MODEL), outs


def kernel(x_prompt, x_sample, cache_attn_k, cache_attn_v, state_ssm, state_ssm_conv, state_sconv, norm_mix_pre, norm_mix_post, norm_mlp_pre, norm_mlp_post, w_in, w_out, attn_norm, ssm_conv_w, ssm_conv_b, ssm_a_log, ssm_dt_bias, ssm_d, ssm_norm, sconv_w, sconv_norm, w_mlp_up, w_mlp_down):
    w = _prep_weights(norm_mix_pre, norm_mix_post, norm_mlp_pre, norm_mlp_post, w_in, w_out,
                      attn_norm, ssm_conv_w, ssm_conv_b, ssm_a_log, ssm_dt_bias, ssm_d,
                      ssm_norm, sconv_w, sconv_norm, w_mlp_up, w_mlp_down)
    y_prompt, (p_k, p_v, p_ssm, p_conv, p_sconv) = _run_trunk(x_prompt, None, w, True)
    states = (cache_attn_k, cache_attn_v, state_ssm, state_ssm_conv, state_sconv)
    y_sample, (s_k, s_v, s_ssm, s_conv, s_sconv) = _run_trunk(x_sample, states, w, False)
    return (y_prompt, y_sample, p_k, p_v, p_ssm, p_conv, p_sconv,
            s_k, s_v, s_ssm, s_conv, s_sconv)
```

```python
import functools

import jax
import jax.numpy as jnp
from jax import lax
from jax.experimental import pallas as pl
from jax.experimental.pallas import tpu as pltpu

F32 = jnp.float32
BF16 = jnp.bfloat16

HEAD_DIM = 64
D_MODEL = 1024
D_A = 256
H_A = 4
D_B = 512
H_B = 8
SSM_GROUPS = 2
N_STATE = 128
CONV_W = 4
CONV_DIM = D_B + 2 * SSM_GROUPS * N_STATE
D_C = 256
SCONV_W = 3
D_FF = 4 * D_MODEL
DILATED_CONFIGS = ((128, 1), (512, 4), (2048, 16))
WIN_MAX = 2048
EPS = 1e-6
LOG2E = 1.4426950408889634
ALIBI_SLOPES = tuple(2.0 ** (-8.0 * (h + 1) / H_A) for h in range(H_A))
ATTN_BLOCK = 128
ATTN_PERM = 16
ATTN_BLOCKS_PER_STEP = 16
ATTN_SAMPLE_GROUP = 2
IN_PROJ_TM = 1024
SSD_CHUNK = 128
SSD_GROUP = 8
SSD_STAGES = 6

COL_XBC = 0
COL_Z = 1024
COL_Q = 1536
COL_G = 2304
COL_DT = 3072
PROJ_W = 3200

VMEM_LIMIT = 56 * 1024 * 1024


def _rms(x, g):
    return x * lax.rsqrt(jnp.mean(x * x, axis=-1, keepdims=True) + EPS) * g


def _silu(x):
    return x * (1.0 / (1.0 + jnp.exp(-x)))


def _softplus(x):
    return jnp.maximum(x, 0.0) + jnp.log1p(jnp.exp(-jnp.abs(x)))


def _in_proj_kernel(x_ref, g_ref, w_ref, *rest, permute):
    o_ref = rest[2] if permute else rest[0]
    h = _rms(x_ref[...], g_ref[...]).astype(BF16)
    for lo, hi in ((0, COL_Q), (COL_Q, COL_G), (COL_G, PROJ_W)):
        res = jnp.dot(h, w_ref[:, lo:hi], preferred_element_type=F32)
        o_ref[:, lo:hi] = res
        if permute and lo == COL_Q:
            operm_ref, pkt_ref, pvt_ref = rest[3:]
            tm = x_ref.shape[0]
            pkt_ref[...] = res[:, D_A:2 * D_A].T
            pvt_ref[...] = res[:, 2 * D_A:3 * D_A].T
            operm_ref[...] = jnp.swapaxes(
                res.reshape(tm // ATTN_PERM, ATTN_PERM, 3 * D_A), 0, 1)


def _in_proj(x2d, g, w, layer, tm, seq=None, wb=None, pkt=None, pvt=None):
    m = x2d.shape[0]
    permute = seq is not None
    in_specs = [
        pl.BlockSpec((tm, D_MODEL), lambda i: (i, 0)),
        pl.BlockSpec((None, 1, D_MODEL), lambda i: (layer, 0, 0)),
        pl.BlockSpec((None, D_MODEL, PROJ_W), lambda i: (layer, 0, 0),
                     pipeline_mode=pl.Buffered(1)),
    ]
    args = [x2d, g, w]
    out_specs = [pl.BlockSpec((tm, PROJ_W), lambda i: (i, 0))]
    out_shape = [jax.ShapeDtypeStruct((m, PROJ_W), F32)]
    scratch = []
    aliases = {}
    if permute:
        tiles = seq // tm
        rows = tm // ATTN_PERM
        first = (seq - wb) // tm
        out_specs.append(pl.BlockSpec((None, ATTN_PERM, rows, 3 * D_A),
                                      lambda i: (i // tiles, 0, i % tiles, 0)))
        out_shape.append(jax.ShapeDtypeStruct((m // seq, ATTN_PERM, seq // ATTN_PERM, 3 * D_A), F32))
        win_spec = pl.BlockSpec((None, None, D_A, tm),
                                lambda i: (layer, i // tiles, 0, jnp.maximum(i % tiles - first, 0)))
        out_specs += [win_spec, win_spec]
        out_shape += [jax.ShapeDtypeStruct(pkt.shape, F32)] * 2
        in_specs += [pl.BlockSpec(memory_space=pl.ANY)] * 2
        args += [pkt, pvt]
        aliases = {3: 2, 4: 3}
    return pl.pallas_call(
        functools.partial(_in_proj_kernel, permute=permute),
        grid=(m // tm,),
        in_specs=in_specs,
        out_specs=out_specs,
        out_shape=out_shape,
        input_output_aliases=aliases,
        scratch_shapes=scratch,
        compiler_params=pltpu.CompilerParams(
            dimension_semantics=("arbitrary",), vmem_limit_bytes=VMEM_LIMIT),
        name="in_proj",
    )(*args)


def _attn_prompt_kernel(q_ref, k_ref, v_ref, g_ref, o0_ref, o1_ref,
                        acc_ref, m_ref, l_ref, bias_ref, *, seq):
    blk = ATTN_BLOCK
    nbr = len(DILATED_CONFIGS)
    P = ATTN_PERM
    per_res = seq // P

    def local_to_strided(a, dil):
        nchunk = P // dil
        rows = blk // nchunk
        return nchunk * (a % rows) + a // rows

    shape4 = (H_A * blk, 2 * blk)
    row4 = lax.broadcasted_iota(jnp.int32, shape4, 0)
    ki = lax.broadcasted_iota(jnp.int32, shape4, 1)
    slope = jnp.zeros(shape4, F32)
    for h in range(H_A):
        slope = jnp.where(row4 // blk == h, ALIBI_SLOPES[h], slope)
    for bi, (win, dil) in enumerate(DILATED_CONFIGS):
        assert win // dil == blk and P % dil == 0
        jq = blk + local_to_strided(row4 % blk, dil)
        jk = local_to_strided(ki % blk, dil) + blk * (ki // blk)
        diff = jq - jk
        band = (diff >= 0) & (diff <= blk)
        dist = slope * (-float(dil) * LOG2E * diff.astype(F32))
        bias_ref[bi] = jnp.where(band, dist, -jnp.inf)
        bias_ref[nbr + bi] = jnp.where(band & (ki >= blk), dist, -jnp.inf)

    lane_head = lax.broadcasted_iota(jnp.int32, (blk, D_A), 1) // HEAD_DIM
    lane_lo = lax.broadcasted_iota(jnp.int32, (blk, 128), 1) < HEAD_DIM

    def per_head(col):
        c = [col[h * blk:(h + 1) * blk, :] for h in range(H_A)]
        return jnp.concatenate([jnp.where(lane_lo, c[0], c[1]), jnp.where(lane_lo, c[2], c[3])],
                               axis=1)

    def blocks(r, u, bi, dil):
        nchunk = P // dil
        rows = blk // nchunk
        nb = min(ATTN_BLOCKS_PER_STEP, per_res // rows)
        starts = [pl.multiple_of(jnp.maximum(nb * u - 1 + j, 0) * rows, rows)
                  for j in range(nb + 1)]

        def load(ref, st):
            return jnp.concatenate(
                [ref[dil * c + r, pl.ds(st, rows), :] for c in range(nchunk)], axis=0)

        def store(ref, st, val):
            for c in range(nchunk):
                ref[dil * c + r, pl.ds(st, rows), :] = val[c * rows:(c + 1) * rows, :]

        kb = [load(k_ref, st).astype(BF16) for st in starts]
        vb = [load(v_ref, st).astype(BF16) for st in starts]
        results = []
        for j in range(nb):
            q = (load(q_ref, starts[1 + j]) * (HEAD_DIM ** -0.5 * LOG2E)).astype(BF16)
            q4 = jnp.concatenate(
                [jnp.where(lane_head == h, q, jnp.zeros_like(q)) for h in range(H_A)], axis=0)
            kk = jnp.concatenate([kb[j], kb[j + 1]], axis=0)
            vv = jnp.concatenate([vb[j], vb[j + 1]], axis=0)
            s = lax.dot_general(q4, kk, (((1,), (1,)), ((), ())), preferred_element_type=F32)
            if j == 0:
                s = s + bias_ref[jnp.where(u == 0, nbr + bi, bi)]
            else:
                s = s + bias_ref[bi]
            m4 = jnp.max(s, axis=-1, keepdims=True)
            p = jnp.exp2(s - m4)
            l4 = jnp.sum(p, axis=-1, keepdims=True)
            pb = p.astype(BF16)
            o = [jnp.dot(pb[h * blk:(h + 1) * blk, :],
                         vv[:, (h // 2) * 128:(h // 2 + 1) * 128],
                         preferred_element_type=F32) for h in range(H_A)]
            acc_b = jnp.concatenate([jnp.where(lane_lo, o[0], o[1]),
                                     jnp.where(lane_lo, o[2], o[3])], axis=1)
            m_b = per_head(m4)
            l_b = per_head(l4)
            if bi > 0:
                st = starts[1 + j]
                m_old = load(m_ref, st)
                m_new = jnp.maximum(m_old, m_b)
                a_old = jnp.exp2(m_old - m_new)
                a_b = jnp.exp2(m_b - m_new)
                acc_b = load(acc_ref, st) * a_old + acc_b * a_b
                l_b = load(l_ref, st) * a_old + l_b * a_b
                m_b = m_new
            results.append((acc_b, m_b, l_b))
        for j in range(nb):
            store(acc_ref, starts[1 + j], results[j][0])
            store(m_ref, starts[1 + j], results[j][1])
            store(l_ref, starts[1 + j], results[j][2])

    for bi, (win, dil) in enumerate(DILATED_CONFIGS):
        nblk = per_res // (blk // (P // dil))
        nb = min(ATTN_BLOCKS_PER_STEP, nblk)
        nstep = nblk // nb
        nres = ATTN_BLOCKS_PER_STEP // nb
        assert dil % nres == 0

        def body(it, carry, bi=bi, dil=dil, nstep=nstep, nres=nres):
            rg = it // nstep
            u = it - rg * nstep
            for rr in range(nres):
                blocks(rg * nres + rr, u, bi, dil)
            return carry

        lax.fori_loop(0, (dil // nres) * nstep, body, 0)

    def finish(r, carry):
        o = _rms(acc_ref[r] / l_ref[r], g_ref[...])
        o0_ref[pl.ds(r, per_res, stride=P), :] = o[:, 0:128]
        o1_ref[pl.ds(r, per_res, stride=P), :] = o[:, 128:256]
        return carry

    lax.fori_loop(0, P, finish, 0)


def _attn_prompt(qkv_perm, g, layer, batch, seq):
    per_res = seq // ATTN_PERM
    qkv_specs = [pl.BlockSpec((None, ATTN_PERM, per_res, D_A),
                              functools.partial(lambda b, j: (b, 0, 0, j), j=j)) for j in range(3)]
    half_spec = pl.BlockSpec((seq, 128), lambda b: (b, 0))
    return pl.pallas_call(
        functools.partial(_attn_prompt_kernel, seq=seq),
        grid=(batch,),
        in_specs=qkv_specs + [pl.BlockSpec((None, 1, D_A), lambda b: (layer, 0, 0))],
        out_specs=[half_spec, half_spec],
        out_shape=[jax.ShapeDtypeStruct((batch * seq, 128), F32)] * 2,
        scratch_shapes=[pltpu.VMEM((ATTN_PERM, per_res, D_A), F32)] * 3
        + [pltpu.VMEM((2 * len(DILATED_CONFIGS), H_A * ATTN_BLOCK, 2 * ATTN_BLOCK), F32)],
        compiler_params=pltpu.CompilerParams(
            dimension_semantics=("parallel",), vmem_limit_bytes=VMEM_LIMIT),
        name="attn_prompt",
    )(qkv_perm, qkv_perm, qkv_perm, g)


def _attn_sample_kernel(q_ref, k_ref, v_ref, ckt_ref, cvt_ref, g_ref, nk_in_ref, nv_in_ref,
                        o0_ref, o1_ref, nkt_ref, nvt_ref, kpad_ref, vpad_ref, *, wb, s_len, group):
    pad = 128
    rows = H_A * s_len
    ncol = wb + pad
    row_i = lax.broadcasted_iota(jnp.int32, (rows, ncol), 0)
    col_i = lax.broadcasted_iota(jnp.int32, (rows, ncol), 1)
    s_i = row_i % s_len
    pos = jnp.where(col_i < wb, col_i, col_i + s_len - pad)
    d = wb + s_i - pos
    real = (col_i < wb) | (col_i >= ncol - s_len)
    mult = jnp.zeros((rows, ncol), F32)
    for win, dil in DILATED_CONFIGS:
        hit = real & (d >= 0) & (d <= win) & ((d % dil) == 0)
        mult = mult + jnp.where(hit, 1.0, 0.0)
    slope = jnp.zeros((rows, ncol), F32)
    for h in range(H_A):
        slope = jnp.where(row_i // s_len == h, ALIBI_SLOPES[h], slope)
    bias = jnp.where(mult > 0.0, -slope * d.astype(F32), -jnp.inf)
    tail = lax.broadcasted_iota(jnp.int32, (D_A, pad), 1) >= pad - s_len
    lane_head = lax.broadcasted_iota(jnp.int32, (s_len, D_A), 1) // HEAD_DIM
    nt = (((1,), (1,)), ((), ()))

    for i in range(group):
        tok = slice(i * s_len, (i + 1) * s_len)
        k_new = k_ref[tok, :]
        v_new = v_ref[tok, :]
        kpad_ref[i] = jnp.zeros(kpad_ref.shape[1:], F32)
        vpad_ref[i] = jnp.zeros(vpad_ref.shape[1:], F32)
        kpad_ref[i, pad - s_len:pad, :] = k_new
        vpad_ref[i, pad - s_len:pad, :] = v_new
        kpad = kpad_ref[i]
        vpad = vpad_ref[i]
        kpad_t = kpad.T
        vpad_t = vpad.T

        for src_ref, dst_ref, new_t in ((ckt_ref, nkt_ref, kpad_t), (cvt_ref, nvt_ref, vpad_t)):
            rolled = pltpu.roll(src_ref[i], wb - s_len, axis=1)
            dst_ref[i, :, 0:wb - pad] = rolled[:, 0:wb - pad]
            dst_ref[i, :, wb - pad:wb] = jnp.where(tail, new_t, rolled[:, wb - pad:wb])

        q = q_ref[tok, :] * (HEAD_DIM ** -0.5)
        qh = jnp.concatenate([jnp.where(lane_head == h, q, 0.0) for h in range(H_A)],
                             axis=0).astype(BF16)
        s1 = jnp.dot(qh, ckt_ref[i].astype(BF16), preferred_element_type=F32)
        s2 = jnp.dot(qh, kpad_t.astype(BF16), preferred_element_type=F32)
        s = jnp.concatenate([s1, s2], axis=1) + bias
        m = jnp.max(s, axis=-1, keepdims=True)
        p = jnp.exp(s - m) * mult
        l = jnp.sum(p, axis=-1, keepdims=True)
        pb = p.astype(BF16)
        o = lax.dot_general(pb[:, 0:wb], cvt_ref[i].astype(BF16), nt, preferred_element_type=F32)
        o = o + jnp.dot(pb[:, wb:ncol], vpad.astype(BF16), preferred_element_type=F32)
        o = o / l
        out = jnp.zeros((s_len, D_A), F32)
        for h in range(H_A):
            out = jnp.where(lane_head == h, o[h * s_len:(h + 1) * s_len, :], out)
        out = _rms(out, g_ref[...])
        o0_ref[tok, :] = out[:, 0:128]
        o1_ref[tok, :] = out[:, 128:256]


def _attn_sample(proj, cache_k, cache_v, g, new_k, new_v, layer, batch, s_len, wb, group):
    depth = cache_k.shape[0]
    qb = COL_Q // D_A
    cache_spec = pl.BlockSpec((None, group, D_A, wb), lambda b: (layer, b, 0, 0))
    rows = group * s_len
    return pl.pallas_call(
        functools.partial(_attn_sample_kernel, wb=wb, s_len=s_len, group=group),
        grid=(batch // group,),
        in_specs=[
            pl.BlockSpec((rows, D_A), lambda b: (b, qb)),
            pl.BlockSpec((rows, D_A), lambda b: (b, qb + 1)),
            pl.BlockSpec((rows, D_A), lambda b: (b, qb + 2)),
            cache_spec, cache_spec,
            pl.BlockSpec((None, 1, D_A), lambda b: (layer, 0, 0)),
            pl.BlockSpec(memory_space=pl.ANY), pl.BlockSpec(memory_space=pl.ANY),
        ],
        out_specs=[pl.BlockSpec((rows, 128), lambda b: (b, 0))] * 2 + [cache_spec, cache_spec],
        out_shape=[jax.ShapeDtypeStruct((batch * s_len, 128), F32)] * 2 + [
                   jax.ShapeDtypeStruct((depth, batch, D_A, wb), F32)] * 2,
        input_output_aliases={6: 2, 7: 3},
        scratch_shapes=[pltpu.VMEM((group, 128, D_A), F32)] * 2,
        compiler_params=pltpu.CompilerParams(
            dimension_semantics=("parallel",), vmem_limit_bytes=VMEM_LIMIT),
        name="attn_sample",
    )(proj, proj, proj, cache_k, cache_v, g, new_k, new_v)


def _ssd_kernel(xbc_ref, z_ref, gate_ref, dtr_ref, h0_ref, cprev_ref, sprev_ref,
                cw_ref, cb_ref, alog_ref, dtb_ref, dsk_ref, sn_ref, scw_ref, scn_ref,
                hall_ref, call_ref, sall_ref,
                y_ref, hout_ref, cnew_ref, snew_ref,
                ext_ref, ext2_ref, h_ref, *, chunk, group):
    c = pl.program_id(1)
    nc = pl.num_programs(1)

    @pl.when(c == 0)
    def _init():
        h_ref[...] = h0_ref[...]
        ext_ref[:, 0:8, :] = cprev_ref[...]
        ext2_ref[:, 0:8, :] = sprev_ref[...]

    for i in range(group):
        _ssd_one(xbc_ref.at[i], z_ref.at[i], gate_ref.at[i], dtr_ref.at[i],
                 cw_ref, cb_ref, alog_ref, dtb_ref, dsk_ref, sn_ref, scw_ref, scn_ref,
                 y_ref.at[i], ext_ref.at[i], ext2_ref.at[i], h_ref.at[i], chunk=chunk)

    @pl.when(c == nc - 1)
    def _final():
        hout_ref[...] = h_ref[...]
        cnew_ref[...] = ext_ref[:, chunk:chunk + 8, :]
        snew_ref[...] = ext2_ref[:, chunk:chunk + 8, :]

    ext_ref[:, 0:8, :] = ext_ref[:, chunk:chunk + 8, :]
    ext2_ref[:, 0:8, :] = ext2_ref[:, chunk:chunk + 8, :]


def _ssd_one(*args, **kwargs):
    for _ in _ssd_stages(*args, **kwargs):
        pass


def _ssd_stages(xbc_ref, z_ref, gate_ref, dtr_ref,
                cw_ref, cb_ref, alog_ref, dtb_ref, dsk_ref, sn_ref, scw_ref, scn_ref,
                y_ref, ext_ref, ext2_ref, h_ref, *, chunk):
    C = chunk
    heads_per_group = H_B // SSM_GROUPS
    gw = heads_per_group * HEAD_DIM

    ext_ref[8:8 + C, :] = xbc_ref[...]
    xc_parts = []
    half = CONV_DIM // 2
    for lo in (0, half):
        conv = xbc_ref[:, lo:lo + half] * cw_ref[CONV_W - 1:CONV_W, lo:lo + half]
        for k in range(CONV_W - 1):
            off = 8 - (CONV_W - 1) + k
            conv = conv + ext_ref[off:off + C, lo:lo + half] * cw_ref[k:k + 1, lo:lo + half]
        xc_parts.append(_silu(conv + cb_ref[:, lo:lo + half]))
        yield
    x_s = xc_parts[0]
    bm = xc_parts[1][:, 0:SSM_GROUPS * N_STATE]
    cm = xc_parts[1][:, SSM_GROUPS * N_STATE:2 * SSM_GROUPS * N_STATE]

    gates = gate_ref[...]
    b_gate = gates[:, 0:D_C]
    prod = gates[:, D_C:2 * D_C] * gates[:, 2 * D_C:3 * D_C]
    ext2_ref[8:8 + C, :] = prod
    sconv = prod * scw_ref[SCONV_W - 1:SCONV_W, :]
    for k in range(SCONV_W - 1):
        off = 8 - (SCONV_W - 1) + k
        sconv = sconv + ext2_ref[off:off + C, :] * scw_ref[k:k + 1, :]
    y_ref[:, D_B:D_B + D_C] = _rms(b_gate * sconv, scn_ref[...]).astype(y_ref.dtype)

    row = lax.broadcasted_iota(jnp.int32, (C, C), 0)
    col = lax.broadcasted_iota(jnp.int32, (C, C), 1)
    causal = row >= col
    dt8 = _softplus((dtr_ref[...] + dtb_ref[...]).T[0:H_B, :])
    a8 = jnp.broadcast_to(-jnp.exp(alog_ref[...]), (128, 128)).T[0:H_B, 0:C]
    da = dt8 * a8
    upper = jnp.where(row <= col, 1.0, 0.0).astype(BF16)
    da_hi = da.astype(BF16)
    rem = da - da_hi.astype(F32)
    da_mid = rem.astype(BF16)
    da_lo = (rem - da_mid.astype(F32)).astype(BF16)
    acs8 = (jnp.dot(da_hi, upper, preferred_element_type=F32)
            + jnp.dot(da_mid, upper, preferred_element_type=F32)
            + jnp.dot(da_lo, upper, preferred_element_type=F32))
    last8 = acs8[:, C - 1:C]
    cdec8 = jnp.exp(last8)
    stacked = jnp.concatenate(
        [acs8, jnp.exp(acs8), jnp.exp(last8 - acs8) * dt8, jnp.zeros((128 - 3 * H_B, C), F32)],
        axis=0)
    cols = stacked.T
    lane_lo = lax.broadcasted_iota(jnp.int32, (C, 2 * HEAD_DIM), 1) < HEAD_DIM

    def expand(base):
        return jnp.concatenate(
            [jnp.where(lane_lo, cols[:, base + 2 * p:base + 2 * p + 1],
                       cols[:, base + 2 * p + 1:base + 2 * p + 2]) for p in range(H_B // 2)],
            axis=1)

    e_acs_x = expand(H_B)
    w_end_x = expand(2 * H_B)
    yield

    nt = (((1,), (1,)), ((), ()))
    tn = (((0,), (0,)), ((), ()))
    x_bf = x_s.astype(BF16)
    xw = (x_s * w_end_x)
    y_parts = []
    for g in range(SSM_GROUPS):
        bm_g = bm[:, g * N_STATE:(g + 1) * N_STATE].astype(BF16)
        cm_g = cm[:, g * N_STATE:(g + 1) * N_STATE].astype(BF16)
        scores = lax.dot_general(cm_g, bm_g, nt, preferred_element_type=F32)
        h_g = h_ref[g * heads_per_group:(g + 1) * heads_per_group].reshape(gw, N_STATE)
        y_off = lax.dot_general(cm_g, h_g.astype(BF16), nt, preferred_element_type=F32)
        y_off = y_off * e_acs_x[:, g * gw:(g + 1) * gw]
        diag = []
        for pair in range(heads_per_group // 2):
            mats = []
            for hh in range(2):
                h = g * heads_per_group + 2 * pair + hh
                seg = cols[:, h:h + 1] - acs8[h:h + 1, :]
                decay = jnp.exp(jnp.where(causal, seg, -jnp.inf))
                mats.append((scores * decay * dt8[h:h + 1, :]).astype(BF16))
            lo = (g * heads_per_group + 2 * pair) * HEAD_DIM
            x_pair = x_bf[:, lo:lo + 2 * HEAD_DIM]
            zero = jnp.zeros_like(x_pair)
            x_blk = jnp.concatenate([jnp.where(lane_lo, x_pair, zero),
                                     jnp.where(lane_lo, zero, x_pair)], axis=0)
            diag.append(jnp.dot(jnp.concatenate(mats, axis=1), x_blk,
                                preferred_element_type=F32))
        y_parts.append(jnp.concatenate(diag, axis=1) + y_off)
        st = lax.dot_general(xw[:, g * gw:(g + 1) * gw].astype(BF16), bm_g, tn,
                             preferred_element_type=F32)
        for hh in range(heads_per_group):
            h = g * heads_per_group + hh
            h_ref[h] = h_ref[h] * cdec8[h:h + 1, 0:1] + st[hh * HEAD_DIM:(hh + 1) * HEAD_DIM, :]
        yield
    y = jnp.concatenate(y_parts, axis=1) + dsk_ref[...] * x_s
    y_ref[:, 0:D_B] = _rms(y * _silu(z_ref[...]), sn_ref[...]).astype(y_ref.dtype)
    yield


def _ssd(proj, h0, cprev8, sprev8, wts, outs, layer, batch, nc, chunk, group):
    (cw, cb, alog, dtb, dsk, sn, scw, scn) = wts
    proj3 = proj.reshape(batch, nc * chunk, PROJ_W)

    def tok(width, blk_idx):
        return pl.BlockSpec((group, chunk, width), lambda b, c: (b, c, blk_idx))

    def per_layer(shape):
        return pl.BlockSpec((None,) + shape, lambda b, c: (layer,) + (0,) * len(shape))

    state_spec = pl.BlockSpec((group, H_B, HEAD_DIM, N_STATE), lambda b, c: (b, 0, 0, 0))
    conv_spec = pl.BlockSpec((group, 8, CONV_DIM), lambda b, c: (b, 0, 0))
    sconv_spec = pl.BlockSpec((group, 8, D_C), lambda b, c: (b, 0, 0))

    def stacked(spec):
        return pl.BlockSpec((None,) + tuple(spec.block_shape),
                            lambda b, c: (layer,) + tuple(spec.index_map(b, c)))

    return pl.pallas_call(
        functools.partial(_ssd_kernel, chunk=chunk, group=group),
        grid=(batch // group, nc),
        in_specs=[
            tok(CONV_DIM, COL_XBC // CONV_DIM),
            tok(D_B, COL_Z // D_B),
            tok(3 * D_C, COL_G // (3 * D_C)),
            tok(128, COL_DT // 128),
            stacked(state_spec), stacked(conv_spec), stacked(sconv_spec),
            per_layer((CONV_W, CONV_DIM)), per_layer((1, CONV_DIM)),
            per_layer((1, 128)), per_layer((1, 128)), per_layer((1, D_B)),
            per_layer((1, D_B)), per_layer((SCONV_W, D_C)), per_layer((1, D_C)),
        ] + [pl.BlockSpec(memory_space=pl.ANY)] * 3,
        out_specs=[
            pl.BlockSpec((group, chunk, D_B + D_C), lambda b, c: (b, c, 0)),
            stacked(state_spec), stacked(conv_spec), stacked(sconv_spec),
        ],
        out_shape=[jax.ShapeDtypeStruct((batch, nc * chunk, D_B + D_C), BF16)]
        + [jax.ShapeDtypeStruct(o.shape, F32) for o in outs],
        input_output_aliases={15: 1, 16: 2, 17: 3},
        scratch_shapes=[
            pltpu.VMEM((group, chunk + 8, CONV_DIM), F32),
            pltpu.VMEM((group, chunk + 8, D_C), F32),
            pltpu.VMEM((group, H_B, HEAD_DIM, N_STATE), F32),
        ],
        compiler_params=pltpu.CompilerParams(
            dimension_semantics=("parallel", "arbitrary"), vmem_limit_bytes=VMEM_LIMIT),
        name="ssd_sconv",
    )(proj3, proj3, proj3, proj3, h0, cprev8, sprev8, cw, cb, alog, dtb, dsk, sn, scw, scn, *outs)


def _post_kernel(x_ref, oa0_ref, oa1_ref, ybc_ref, woa_ref, wobc_ref, gpost_ref, gpre_ref,
                 gmlp_ref, wup_ref, wdn_ref, o_ref, acc_ref, *, tf):
    oa = jnp.concatenate([oa0_ref[...], oa1_ref[...]], axis=1).astype(BF16)
    mix = jnp.dot(oa, woa_ref[...], preferred_element_type=F32)
    mix = mix + jnp.dot(ybc_ref[...], wobc_ref[...], preferred_element_type=F32)
    x1 = x_ref[...] + _rms(mix, gpost_ref[...])
    h = _rms(x1, gpre_ref[...]).astype(BF16)
    for j in range(D_FF // tf):
        u = jnp.dot(h, wup_ref[:, j * tf:(j + 1) * tf], preferred_element_type=F32)
        u = jnp.square(jnp.maximum(u, 0.0)).astype(BF16)
        f = jnp.dot(u, wdn_ref[j * tf:(j + 1) * tf, :], preferred_element_type=F32)
        if j == 0:
            acc_ref[...] = f
        else:
            acc_ref[...] += f
    o_ref[...] = x1 + _rms(acc_ref[...], gmlp_ref[...])


def _post(x2d, oa0, oa1, ybc, w_out_a, w_out_bc, gpost, gpre, gmlp, w_up, w_dn, layer, tm, tf):
    m = x2d.shape[0]

    def resident(shape):
        return pl.BlockSpec((None,) + shape, lambda i: (layer, 0, 0),
                            pipeline_mode=pl.Buffered(1))

    return pl.pallas_call(
        functools.partial(_post_kernel, tf=tf),
        grid=(m // tm,),
        in_specs=[
            pl.BlockSpec((tm, D_MODEL), lambda i: (i, 0)),
            pl.BlockSpec((tm, 128), lambda i: (i, 0)),
            pl.BlockSpec((tm, 128), lambda i: (i, 0)),
            pl.BlockSpec((tm, D_B + D_C), lambda i: (i, 0)),
            resident((D_A, D_MODEL)), resident((D_B + D_C, D_MODEL)),
            resident((1, D_MODEL)), resident((1, D_MODEL)), resident((1, D_MODEL)),
            resident((D_MODEL, D_FF)), resident((D_FF, D_MODEL)),
        ],
        out_specs=pl.BlockSpec((tm, D_MODEL), lambda i: (i, 0)),
        out_shape=jax.ShapeDtypeStruct((m, D_MODEL), F32),
        scratch_shapes=[pltpu.VMEM((tm, D_MODEL), F32)],
        compiler_params=pltpu.CompilerParams(
            dimension_semantics=("parallel",), vmem_limit_bytes=VMEM_LIMIT),
        name="post",
    )(x2d, oa0, oa1, ybc, w_out_a, w_out_bc, gpost, gpre, gmlp, w_up, w_dn)


def _fused_kernel(x_ref, oa0_ref, oa1_ref, woa_ref, wobc_ref, gpost_ref, gpre_ref, gmlp_ref,
                  wup_ref, wdn_ref,
                  xbc_ref, z_ref, gate_ref, dtr_ref,
                  cw_ref, cb_ref, alog_ref, dtb_ref, dsk_ref, sn_ref, scw_ref, scn_ref,
                  hall_ref, call_ref, sall_ref,
                  o_ref, hout_ref, cnew_ref, snew_ref,
                  acc_ref, ybc_ref, ext_ref, ext2_ref, h_ref, *, tf, tm, tiles_per_seq, ntiles):
    i = pl.program_id(0)
    slot = i % 2
    nchunks = tm // SSD_CHUNK

    @pl.when(i == 0)
    def _first():
        ybc_ref[...] = jnp.zeros(ybc_ref.shape, ybc_ref.dtype)

    @pl.when(i % tiles_per_seq == 0)
    def _new_sequence():
        h_ref[...] = jnp.zeros(h_ref.shape, F32)
        ext_ref[0:8, :] = jnp.zeros((8, CONV_DIM), F32)
        ext2_ref[0:8, :] = jnp.zeros((8, D_C), F32)

    def ssd_chunk(c):
        rows = slice(c * SSD_CHUNK, (c + 1) * SSD_CHUNK)
        yield from _ssd_stages(
            xbc_ref.at[rows], z_ref.at[rows], gate_ref.at[rows], dtr_ref.at[rows],
            cw_ref, cb_ref, alog_ref, dtb_ref, dsk_ref, sn_ref, scw_ref, scn_ref,
            ybc_ref.at[slot, rows], ext_ref, ext2_ref, h_ref, chunk=SSD_CHUNK)
        ext_ref[0:8, :] = ext_ref[SSD_CHUNK:SSD_CHUNK + 8, :]
        ext2_ref[0:8, :] = ext2_ref[SSD_CHUNK:SSD_CHUNK + 8, :]

    def ssd_tile():
        for c in range(nchunks):
            yield from ssd_chunk(c)

    stages = ssd_tile()
    nff = D_FF // tf
    per_ff = -(-(nchunks * SSD_STAGES) // (nff + 1))

    def advance():
        for _ in range(per_ff):
            next(stages, None)

    oa = jnp.concatenate([oa0_ref[...], oa1_ref[...]], axis=1).astype(BF16)
    mix = jnp.dot(oa, woa_ref[...], preferred_element_type=F32)
    mix = mix + jnp.dot(ybc_ref[1 - slot], wobc_ref[...], preferred_element_type=F32)
    x1 = x_ref[...] + _rms(mix, gpost_ref[...])
    h = _rms(x1, gpre_ref[...]).astype(BF16)
    advance()
    for j in range(nff):
        u = jnp.dot(h, wup_ref[:, j * tf:(j + 1) * tf], preferred_element_type=F32)
        u = jnp.square(jnp.maximum(u, 0.0)).astype(BF16)
        f = jnp.dot(u, wdn_ref[j * tf:(j + 1) * tf, :], preferred_element_type=F32)
        if j == 0:
            acc_ref[...] = f
        else:
            acc_ref[...] += f
        advance()
    for _ in stages:
        pass
    o_ref[...] = x1 + _rms(acc_ref[...], gmlp_ref[...])

    @pl.when((i % tiles_per_seq == tiles_per_seq - 1) & (i < ntiles))
    def _sequence_done():
        hout_ref[...] = h_ref[...]
        cnew_ref[...] = ext_ref[0:8, :]
        snew_ref[...] = ext2_ref[0:8, :]


def _fused(x2d, oa0, oa1, proj, w_out_a, w_out_bc, gpost, gpre, gmlp, w_up, w_dn, wts, outs,
           layer, batch, tm, tf):
    m = x2d.shape[0]
    (cw, cb, alog, dtb, dsk, sn, scw, scn) = wts
    ntiles = m // tm
    tps = ntiles // batch

    def resident(shape):
        return pl.BlockSpec((None,) + shape, lambda i: (layer,) + (0,) * len(shape),
                            pipeline_mode=pl.Buffered(1))

    def cur(width, blk_idx):
        return pl.BlockSpec((tm, width), lambda i: (jnp.minimum(i, ntiles - 1), blk_idx))

    def prev(width):
        return pl.BlockSpec((tm, width), lambda i: (jnp.maximum(i - 1, 0), 0))

    def per_seq(shape):
        return pl.BlockSpec((None, None) + shape,
                            lambda i: (layer, jnp.minimum(i, ntiles - 1) // tps) + (0,) * len(shape))

    return pl.pallas_call(
        functools.partial(_fused_kernel, tf=tf, tm=tm, tiles_per_seq=tps, ntiles=ntiles),
        grid=(ntiles + 1,),
        in_specs=[
            prev(D_MODEL), prev(128), prev(128),
            resident((D_A, D_MODEL)), resident((D_B + D_C, D_MODEL)),
            resident((1, D_MODEL)), resident((1, D_MODEL)), resident((1, D_MODEL)),
            resident((D_MODEL, D_FF)), resident((D_FF, D_MODEL)),
            cur(CONV_DIM, COL_XBC // CONV_DIM), cur(D_B, COL_Z // D_B),
            cur(3 * D_C, COL_G // (3 * D_C)), cur(128, COL_DT // 128),
            resident((CONV_W, CONV_DIM)), resident((1, CONV_DIM)),
            resident((1, 128)), resident((1, 128)), resident((1, D_B)),
            resident((1, D_B)), resident((SCONV_W, D_C)), resident((1, D_C)),
        ] + [pl.BlockSpec(memory_space=pl.ANY)] * 3,
        out_specs=[prev(D_MODEL), per_seq((H_B, HEAD_DIM, N_STATE)), per_seq((8, CONV_DIM)),
                   per_seq((8, D_C))],
        out_shape=[jax.ShapeDtypeStruct((m, D_MODEL), F32)]
        + [jax.ShapeDtypeStruct(o.shape, F32) for o in outs],
        input_output_aliases={22: 1, 23: 2, 24: 3},
        scratch_shapes=[pltpu.VMEM((tm, D_MODEL), F32),
                        pltpu.VMEM((2, tm, D_B + D_C), BF16),
                        pltpu.VMEM((SSD_CHUNK + 8, CONV_DIM), F32),
                        pltpu.VMEM((SSD_CHUNK + 8, D_C), F32),
                        pltpu.VMEM((H_B, HEAD_DIM, N_STATE), F32)],
        compiler_params=pltpu.CompilerParams(
            dimension_semantics=("arbitrary",), vmem_limit_bytes=VMEM_LIMIT),
        name="ssd_post",
    )(x2d, oa0, oa1, w_out_a, w_out_bc, gpost, gpre, gmlp, w_up, w_dn,
      proj, proj, proj, proj, cw, cb, alog, dtb, dsk, sn, scw, scn, *outs)


def _pack_w_in_kernel(wt_ref, o_ref, *, cols):
    piece = 128
    moves = []
    for name, dst in (("xbc", COL_XBC), ("z", COL_Z), ("q", COL_Q), ("k", COL_Q + D_A),
                      ("v", COL_Q + 2 * D_A), ("b", COL_G), ("c", COL_G + D_C),
                      ("u", COL_G + 2 * D_C)):
        lo, hi = cols[name]
        moves += [(lo + j, dst + j) for j in range(0, hi - lo, piece)]
    for src, dst in moves:
        o_ref[:, dst:dst + piece] = wt_ref[src:src + piece, :].T.astype(BF16)
    lo = cols["dt"][0]
    blk = wt_ref[lo:lo + piece, :].T
    lane = lax.broadcasted_iota(jnp.int32, blk.shape, 1)
    o_ref[:, COL_DT:COL_DT + piece] = jnp.where(lane < H_B, blk, 0.0).astype(BF16)


def _pack_w_in(w_in, cols):
    depth, k, n = w_in.shape
    w_t = jnp.transpose(w_in, (0, 2, 1))
    return pl.pallas_call(
        functools.partial(_pack_w_in_kernel, cols=cols),
        grid=(depth,),
        in_specs=[pl.BlockSpec((None, n, k), lambda l: (l, 0, 0))],
        out_specs=pl.BlockSpec((None, k, PROJ_W), lambda l: (l, 0, 0)),
        out_shape=jax.ShapeDtypeStruct((depth, k, PROJ_W), BF16),
        compiler_params=pltpu.CompilerParams(
            dimension_semantics=("parallel",), vmem_limit_bytes=VMEM_LIMIT),
        name="pack_w_in",
    )(w_t)


def _prep_weights(norm_mix_pre, norm_mix_post, norm_mlp_pre, norm_mlp_post, w_in, w_out,
                  attn_norm, ssm_conv_w, ssm_conv_b, ssm_a_log, ssm_dt_bias, ssm_d, ssm_norm,
                  sconv_w, sconv_norm, w_mlp_up, w_mlp_down):
    depth = w_in.shape[0]
    o = 0
    cols = {}
    for name, width in (("q", D_A), ("k", D_A), ("v", D_A), ("z", D_B), ("xbc", CONV_DIM),
                        ("dt", H_B), ("b", D_C), ("c", D_C), ("u", D_C)):
        cols[name] = (o, o + width)
        o += width

    w_in_r = _pack_w_in(w_in, cols)

    def lane_pad(v):
        return jnp.pad(v.astype(F32), ((0, 0), (0, 128 - H_B)))[:, None, :]

    def row(v):
        return v.astype(F32)[:, None, :]

    return dict(
        g_mix_pre=row(norm_mix_pre), g_mix_post=row(norm_mix_post),
        g_mlp_pre=row(norm_mlp_pre), g_mlp_post=row(norm_mlp_post),
        w_in=w_in_r,
        w_out_a=w_out[:, 0:D_A, :].astype(BF16),
        w_out_bc=w_out[:, D_A:, :].astype(BF16),
        attn_norm=row(attn_norm),
        ssd=(ssm_conv_w.astype(F32), row(ssm_conv_b), lane_pad(ssm_a_log),
             lane_pad(ssm_dt_bias), row(jnp.repeat(ssm_d, HEAD_DIM, axis=1)),
             row(ssm_norm), sconv_w.astype(F32), row(sconv_norm)),
        w_up=w_mlp_up.astype(BF16), w_dn=w_mlp_down.astype(BF16),
    )


def _run_trunk(x, states, w, prompt):
    batch, seq, _ = x.shape
    depth = w["w_in"].shape[0]
    m = batch * seq
    x2d = x.reshape(m, D_MODEL)
    tm = 512 if m % 512 == 0 else m
    if prompt:
        wb = min(WIN_MAX, seq)
    else:
        cache_k, cache_v, st_ssm, st_conv, st_sconv = states
        wb = cache_k.shape[2]
        cache_k = jnp.transpose(cache_k, (0, 1, 3, 4, 2)).reshape(depth, batch, D_A, wb)
        cache_v = jnp.transpose(cache_v, (0, 1, 3, 4, 2)).reshape(depth, batch, D_A, wb)
        h0 = st_ssm.astype(F32)
        cprev8 = jnp.pad(st_conv.astype(F32), ((0, 0), (0, 0), (8 - (CONV_W - 1), 0), (0, 0)))
        sprev8 = jnp.pad(st_sconv.astype(F32), ((0, 0), (0, 0), (8 - (SCONV_W - 1), 0), (0, 0)))
    new_k = lax.empty((depth, batch, D_A, wb), F32)
    new_v = lax.empty((depth, batch, D_A, wb), F32)
    st_out = (lax.empty((depth, batch, H_B, HEAD_DIM, N_STATE), F32),
              lax.empty((depth, batch, 8, CONV_DIM), F32),
              lax.empty((depth, batch, 8, D_C), F32))
    for l in range(depth):
        if prompt:
            proj, qkv_perm, new_k, new_v = _in_proj(x2d, w["g_mix_pre"], w["w_in"], l,
                                                    IN_PROJ_TM if seq % IN_PROJ_TM == 0 else tm,
                                                    seq, wb, new_k, new_v)
            oa0, oa1 = _attn_prompt(qkv_perm, w["attn_norm"], l, batch, seq)
            x2d, *st_out = _fused(x2d, oa0, oa1, proj, w["w_out_a"], w["w_out_bc"],
                                  w["g_mix_post"], w["g_mlp_pre"], w["g_mlp_post"],
                                  w["w_up"], w["w_dn"], w["ssd"], st_out, l, batch, tm, 512)
        else:
            (proj,) = _in_proj(x2d, w["g_mix_pre"], w["w_in"], l, tm)
            oa0, oa1, new_k, new_v = _attn_sample(
                proj, cache_k, cache_v, w["attn_norm"], new_k, new_v, l, batch, seq, wb,
                ATTN_SAMPLE_GROUP if batch % ATTN_SAMPLE_GROUP == 0 else 1)
            ybc, *st_out = _ssd(proj, h0, cprev8, sprev8, w["ssd"], st_out, l, batch, 1, seq,
                                SSD_GROUP if batch % SSD_GROUP == 0 else 1)
            ybc = ybc.reshape(m, D_B + D_C)
            x2d = _post(x2d, oa0, oa1, ybc, w["w_out_a"], w["w_out_bc"], w["g_mix_post"],
                        w["g_mlp_pre"], w["g_mlp_post"], w["w_up"], w["w_dn"], l, tm, 1024)
    h_all, c_all, s_all = st_out
    outs = (jnp.transpose(new_k.reshape(depth, batch, H_A, HEAD_DIM, wb), (0, 1, 4, 2, 3)),
            jnp.transpose(new_v.reshape(depth, batch, H_A, HEAD_DIM, wb), (0, 1, 4, 2, 3)),
            h_all, c_all[:, :, 8 - (CONV_W - 1):, :], s_all[:, :, 8 - (SCONV_W - 1):, :])
    return x2d.reshape(batch, seq, D_MODEL), outs


def kernel(x_prompt, x_sample, cache_attn_k, cache_attn_v, state_ssm, state_ssm_conv, state_sconv, norm_mix_pre, norm_mix_post, norm_mlp_pre, norm_mlp_post, w_in, w_out, attn_norm, ssm_conv_w, ssm_conv_b, ssm_a_log, ssm_dt_bias, ssm_d, ssm_norm, sconv_w, sconv_norm, w_mlp_up, w_mlp_down):
    w = _prep_weights(norm_mix_pre, norm_mix_post, norm_mlp_pre, norm_mlp_post, w_in, w_out,
                      attn_norm, ssm_conv_w, ssm_conv_b, ssm_a_log, ssm_dt_bias, ssm_d,
                      ssm_norm, sconv_w, sconv_norm, w_mlp_up, w_mlp_down)
    y_prompt, (p_k, p_v, p_ssm, p_conv, p_sconv) = _run_trunk(x_prompt, None, w, True)
    states = (cache_attn_k, cache_attn_v, state_ssm, state_ssm_conv, state_sconv)
    y_sample, (s_k, s_v, s_ssm, s_conv, s_sconv) = _run_trunk(x_sample, states, w, False)
    return (y_prompt, y_sample, p_k, p_v, p_ssm, p_conv, p_sconv,
            s_k, s_v, s_ssm, s_conv, s_sconv)
```

```python
import functools

import jax
import jax.numpy as jnp
from jax import lax
from jax.experimental import pallas as pl
from jax.experimental.pallas import tpu as pltpu

F32 = jnp.float32
BF16 = jnp.bfloat16

HEAD_DIM = 64
D_MODEL = 1024
D_A = 256
H_A = 4
D_B = 512
H_B = 8
SSM_GROUPS = 2
N_STATE = 128
CONV_W = 4
CONV_DIM = D_B + 2 * SSM_GROUPS * N_STATE
D_C = 256
SCONV_W = 3
D_FF = 4 * D_MODEL
DILATED_CONFIGS = ((128, 1), (512, 4), (2048, 16))
WIN_MAX = 2048
EPS = 1e-6
LOG2E = 1.4426950408889634
ALIBI_SLOPES = tuple(2.0 ** (-8.0 * (h + 1) / H_A) for h in range(H_A))
ATTN_BLOCK = 128
ATTN_PERM = 16
ATTN_BLOCKS_PER_STEP = 16
ATTN_SAMPLE_GROUP = 2
IN_PROJ_TM = 1024
SSD_CHUNK = 128
SSD_GROUP = 8
UP_GROUP = 2
SSD_STAGES = 6

COL_XBC = 0
COL_Z = 1024
COL_Q = 1536
COL_G = 2304
COL_DT = 3072
PROJ_W = 3200

VMEM_LIMIT = 56 * 1024 * 1024


def _rms(x, g):
    return x * lax.rsqrt(jnp.mean(x * x, axis=-1, keepdims=True) + EPS) * g


def _silu(x):
    return x * (1.0 / (1.0 + jnp.exp(-x)))


def _softplus(x):
    return jnp.maximum(x, 0.0) + jnp.log1p(jnp.exp(-jnp.abs(x)))


def _in_proj_kernel(x_ref, g_ref, w_ref, *rest, permute):
    o_ref = rest[2] if permute else rest[0]
    h = _rms(x_ref[...], g_ref[...]).astype(BF16)
    for lo, hi in ((0, COL_Q), (COL_Q, COL_G), (COL_G, PROJ_W)):
        res = jnp.dot(h, w_ref[:, lo:hi], preferred_element_type=F32)
        o_ref[:, lo:hi] = res
        if permute and lo == COL_Q:
            operm_ref, pkt_ref, pvt_ref = rest[3:]
            tm = x_ref.shape[0]
            pkt_ref[...] = res[:, D_A:2 * D_A].T
            pvt_ref[...] = res[:, 2 * D_A:3 * D_A].T
            operm_ref[...] = jnp.swapaxes(
                res.reshape(tm // ATTN_PERM, ATTN_PERM, 3 * D_A), 0, 1)


def _in_proj(x2d, g, w, layer, tm, seq=None, wb=None, pkt=None, pvt=None):
    m = x2d.shape[0]
    permute = seq is not None
    in_specs = [
        pl.BlockSpec((tm, D_MODEL), lambda i: (i, 0)),
        pl.BlockSpec((None, 1, D_MODEL), lambda i: (layer, 0, 0)),
        pl.BlockSpec((None, D_MODEL, PROJ_W), lambda i: (layer, 0, 0),
                     pipeline_mode=pl.Buffered(1)),
    ]
    args = [x2d, g, w]
    out_specs = [pl.BlockSpec((tm, PROJ_W), lambda i: (i, 0))]
    out_shape = [jax.ShapeDtypeStruct((m, PROJ_W), F32)]
    scratch = []
    aliases = {}
    if permute:
        tiles = seq // tm
        rows = tm // ATTN_PERM
        first = (seq - wb) // tm
        out_specs.append(pl.BlockSpec((None, ATTN_PERM, rows, 3 * D_A),
                                      lambda i: (i // tiles, 0, i % tiles, 0)))
        out_shape.append(jax.ShapeDtypeStruct((m // seq, ATTN_PERM, seq // ATTN_PERM, 3 * D_A), F32))
        win_spec = pl.BlockSpec((None, None, D_A, tm),
                                lambda i: (layer, i // tiles, 0, jnp.maximum(i % tiles - first, 0)))
        out_specs += [win_spec, win_spec]
        out_shape += [jax.ShapeDtypeStruct(pkt.shape, F32)] * 2
        in_specs += [pl.BlockSpec(memory_space=pl.ANY)] * 2
        args += [pkt, pvt]
        aliases = {3: 2, 4: 3}
    return pl.pallas_call(
        functools.partial(_in_proj_kernel, permute=permute),
        grid=(m // tm,),
        in_specs=in_specs,
        out_specs=out_specs,
        out_shape=out_shape,
        input_output_aliases=aliases,
        scratch_shapes=scratch,
        compiler_params=pltpu.CompilerParams(
            dimension_semantics=("arbitrary",), vmem_limit_bytes=VMEM_LIMIT),
        name="in_proj",
    )(*args)


def _attn_prompt_kernel(q_ref, k_ref, v_ref, g_ref, o0_ref, o1_ref,
                        acc_ref, m_ref, l_ref, bias_ref, *, seq):
    blk = ATTN_BLOCK
    nbr = len(DILATED_CONFIGS)
    P = ATTN_PERM
    per_res = seq // P

    def local_to_strided(a, dil):
        nchunk = P // dil
        rows = blk // nchunk
        return nchunk * (a % rows) + a // rows

    shape4 = (H_A * blk, 2 * blk)
    row4 = lax.broadcasted_iota(jnp.int32, shape4, 0)
    ki = lax.broadcasted_iota(jnp.int32, shape4, 1)
    slope = jnp.zeros(shape4, F32)
    for h in range(H_A):
        slope = jnp.where(row4 // blk == h, ALIBI_SLOPES[h], slope)
    for bi, (win, dil) in enumerate(DILATED_CONFIGS):
        assert win // dil == blk and P % dil == 0
        jq = blk + local_to_strided(row4 % blk, dil)
        jk = local_to_strided(ki % blk, dil) + blk * (ki // blk)
        diff = jq - jk
        band = (diff >= 0) & (diff <= blk)
        dist = slope * (-float(dil) * LOG2E * diff.astype(F32))
        bias_ref[bi] = jnp.where(band, dist, -jnp.inf)
        bias_ref[nbr + bi] = jnp.where(band & (ki >= blk), dist, -jnp.inf)

    lane_head = lax.broadcasted_iota(jnp.int32, (blk, D_A), 1) // HEAD_DIM
    lane_lo = lax.broadcasted_iota(jnp.int32, (blk, 128), 1) < HEAD_DIM

    def per_head(col):
        c = [col[h * blk:(h + 1) * blk, :] for h in range(H_A)]
        return jnp.concatenate([jnp.where(lane_lo, c[0], c[1]), jnp.where(lane_lo, c[2], c[3])],
                               axis=1)

    def blocks(r, u, bi, dil):
        nchunk = P // dil
        rows = blk // nchunk
        nb = min(ATTN_BLOCKS_PER_STEP, per_res // rows)
        starts = [pl.multiple_of(jnp.maximum(nb * u - 1 + j, 0) * rows, rows)
                  for j in range(nb + 1)]

        def load(ref, st):
            return jnp.concatenate(
                [ref[dil * c + r, pl.ds(st, rows), :] for c in range(nchunk)], axis=0)

        def store(ref, st, val):
            for c in range(nchunk):
                ref[dil * c + r, pl.ds(st, rows), :] = val[c * rows:(c + 1) * rows, :]

        kb = [load(k_ref, st).astype(BF16) for st in starts]
        vb = [load(v_ref, st).astype(BF16) for st in starts]
        results = []
        for j in range(nb):
            q = (load(q_ref, starts[1 + j]) * (HEAD_DIM ** -0.5 * LOG2E)).astype(BF16)
            q4 = jnp.concatenate(
                [jnp.where(lane_head == h, q, jnp.zeros_like(q)) for h in range(H_A)], axis=0)
            kk = jnp.concatenate([kb[j], kb[j + 1]], axis=0)
            vv = jnp.concatenate([vb[j], vb[j + 1]], axis=0)
            s = lax.dot_general(q4, kk, (((1,), (1,)), ((), ())), preferred_element_type=F32)
            if j == 0:
                s = s + bias_ref[jnp.where(u == 0, nbr + bi, bi)]
            else:
                s = s + bias_ref[bi]
            m4 = jnp.max(s, axis=-1, keepdims=True)
            p = jnp.exp2(s - m4)
            l4 = jnp.sum(p, axis=-1, keepdims=True)
            pb = p.astype(BF16)
            o = [jnp.dot(pb[h * blk:(h + 1) * blk, :],
                         vv[:, (h // 2) * 128:(h // 2 + 1) * 128],
                         preferred_element_type=F32) for h in range(H_A)]
            acc_b = jnp.concatenate([jnp.where(lane_lo, o[0], o[1]),
                                     jnp.where(lane_lo, o[2], o[3])], axis=1)
            m_b = per_head(m4)
            l_b = per_head(l4)
            if bi > 0:
                st = starts[1 + j]
                m_old = load(m_ref, st)
                m_new = jnp.maximum(m_old, m_b)
                a_old = jnp.exp2(m_old - m_new)
                a_b = jnp.exp2(m_b - m_new)
                acc_b = load(acc_ref, st) * a_old + acc_b * a_b
                l_b = load(l_ref, st) * a_old + l_b * a_b
                m_b = m_new
            results.append((acc_b, m_b, l_b))
        for j in range(nb):
            if bi < nbr - 1:
                store(acc_ref, starts[1 + j], results[j][0])
                store(m_ref, starts[1 + j], results[j][1])
                store(l_ref, starts[1 + j], results[j][2])
            else:
                assert dil == P and nchunk == 1
                o = _rms(results[j][0] / results[j][2], g_ref[...])
                out_rows = pl.ds(r + P * starts[1 + j], blk, stride=P)
                o0_ref[out_rows, :] = o[:, 0:128]
                o1_ref[out_rows, :] = o[:, 128:256]

    for bi, (win, dil) in enumerate(DILATED_CONFIGS):
        nblk = per_res // (blk // (P // dil))
        nb = min(ATTN_BLOCKS_PER_STEP, nblk)
        nstep = nblk // nb
        nres = ATTN_BLOCKS_PER_STEP // nb
        assert dil % nres == 0

        def body(it, carry, bi=bi, dil=dil, nstep=nstep, nres=nres):
            rg = it // nstep
            u = it - rg * nstep
            for rr in range(nres):
                blocks(rg * nres + rr, u, bi, dil)
            return carry

        lax.fori_loop(0, (dil // nres) * nstep, body, 0)


def _attn_prompt(qkv_perm, g, layer, batch, seq):
    per_res = seq // ATTN_PERM
    qkv_specs = [pl.BlockSpec((None, ATTN_PERM, per_res, D_A),
                              functools.partial(lambda b, j: (b, 0, 0, j), j=j)) for j in range(3)]
    half_spec = pl.BlockSpec((seq, 128), lambda b: (b, 0))
    return pl.pallas_call(
        functools.partial(_attn_prompt_kernel, seq=seq),
        grid=(batch,),
        in_specs=qkv_specs + [pl.BlockSpec((None, 1, D_A), lambda b: (layer, 0, 0))],
        out_specs=[half_spec, half_spec],
        out_shape=[jax.ShapeDtypeStruct((batch * seq, 128), F32)] * 2,
        scratch_shapes=[pltpu.VMEM((ATTN_PERM, per_res, D_A), F32)] * 3
        + [pltpu.VMEM((2 * len(DILATED_CONFIGS), H_A * ATTN_BLOCK, 2 * ATTN_BLOCK), F32)],
        compiler_params=pltpu.CompilerParams(
            dimension_semantics=("parallel",), vmem_limit_bytes=VMEM_LIMIT),
        name="attn_prompt",
    )(qkv_perm, qkv_perm, qkv_perm, g)


def _attn_sample_kernel(q_ref, k_ref, v_ref, ckt_ref, cvt_ref, g_ref, nk_in_ref, nv_in_ref,
                        o0_ref, o1_ref, nkt_ref, nvt_ref, kpad_ref, vpad_ref, *, wb, s_len, group):
    pad = 128
    rows = H_A * s_len
    ncol = wb + pad
    row_i = lax.broadcasted_iota(jnp.int32, (rows, ncol), 0)
    col_i = lax.broadcasted_iota(jnp.int32, (rows, ncol), 1)
    s_i = row_i % s_len
    pos = jnp.where(col_i < wb, col_i, col_i + s_len - pad)
    d = wb + s_i - pos
    real = (col_i < wb) | (col_i >= ncol - s_len)
    mult = jnp.zeros((rows, ncol), F32)
    for win, dil in DILATED_CONFIGS:
        hit = real & (d >= 0) & (d <= win) & ((d % dil) == 0)
        mult = mult + jnp.where(hit, 1.0, 0.0)
    slope = jnp.zeros((rows, ncol), F32)
    for h in range(H_A):
        slope = jnp.where(row_i // s_len == h, ALIBI_SLOPES[h], slope)
    bias = jnp.where(mult > 0.0, -slope * d.astype(F32), -jnp.inf)
    tail = lax.broadcasted_iota(jnp.int32, (D_A, pad), 1) >= pad - s_len
    lane_head = lax.broadcasted_iota(jnp.int32, (s_len, D_A), 1) // HEAD_DIM
    nt = (((1,), (1,)), ((), ()))

    for i in range(group):
        tok = slice(i * s_len, (i + 1) * s_len)
        k_new = k_ref[tok, :]
        v_new = v_ref[tok, :]
        kpad_ref[i] = jnp.zeros(kpad_ref.shape[1:], F32)
        vpad_ref[i] = jnp.zeros(vpad_ref.shape[1:], F32)
        kpad_ref[i, pad - s_len:pad, :] = k_new
        vpad_ref[i, pad - s_len:pad, :] = v_new
        kpad = kpad_ref[i]
        vpad = vpad_ref[i]
        kpad_t = kpad.T
        vpad_t = vpad.T

        for src_ref, dst_ref, new_t in ((ckt_ref, nkt_ref, kpad_t), (cvt_ref, nvt_ref, vpad_t)):
            rolled = pltpu.roll(src_ref[i], wb - s_len, axis=1)
            dst_ref[i, :, 0:wb - pad] = rolled[:, 0:wb - pad]
            dst_ref[i, :, wb - pad:wb] = jnp.where(tail, new_t, rolled[:, wb - pad:wb])

        q = q_ref[tok, :] * (HEAD_DIM ** -0.5)
        qh = jnp.concatenate([jnp.where(lane_head == h, q, 0.0) for h in range(H_A)],
                             axis=0).astype(BF16)
        s1 = jnp.dot(qh, ckt_ref[i].astype(BF16), preferred_element_type=F32)
        s2 = jnp.dot(qh, kpad_t.astype(BF16), preferred_element_type=F32)
        s = jnp.concatenate([s1, s2], axis=1) + bias
        m = jnp.max(s, axis=-1, keepdims=True)
        p = jnp.exp(s - m) * mult
        l = jnp.sum(p, axis=-1, keepdims=True)
        pb = p.astype(BF16)
        o = lax.dot_general(pb[:, 0:wb], cvt_ref[i].astype(BF16), nt, preferred_element_type=F32)
        o = o + jnp.dot(pb[:, wb:ncol], vpad.astype(BF16), preferred_element_type=F32)
        o = o / l
        out = jnp.zeros((s_len, D_A), F32)
        for h in range(H_A):
            out = jnp.where(lane_head == h, o[h * s_len:(h + 1) * s_len, :], out)
        out = _rms(out, g_ref[...])
        o0_ref[tok, :] = out[:, 0:128]
        o1_ref[tok, :] = out[:, 128:256]


def _attn_sample(proj, cache_k, cache_v, g, new_k, new_v, layer, batch, s_len, wb, group):
    depth = cache_k.shape[0]
    qb = COL_Q // D_A
    cache_spec = pl.BlockSpec((None, group, D_A, wb), lambda b: (layer, b, 0, 0))
    rows = group * s_len
    return pl.pallas_call(
        functools.partial(_attn_sample_kernel, wb=wb, s_len=s_len, group=group),
        grid=(batch // group,),
        in_specs=[
            pl.BlockSpec((rows, D_A), lambda b: (b, qb)),
            pl.BlockSpec((rows, D_A), lambda b: (b, qb + 1)),
            pl.BlockSpec((rows, D_A), lambda b: (b, qb + 2)),
            cache_spec, cache_spec,
            pl.BlockSpec((None, 1, D_A), lambda b: (layer, 0, 0)),
            pl.BlockSpec(memory_space=pl.ANY), pl.BlockSpec(memory_space=pl.ANY),
        ],
        out_specs=[pl.BlockSpec((rows, 128), lambda b: (b, 0))] * 2 + [cache_spec, cache_spec],
        out_shape=[jax.ShapeDtypeStruct((batch * s_len, 128), F32)] * 2 + [
                   jax.ShapeDtypeStruct((depth, batch, D_A, wb), F32)] * 2,
        input_output_aliases={6: 2, 7: 3},
        scratch_shapes=[pltpu.VMEM((group, 128, D_A), F32)] * 2,
        compiler_params=pltpu.CompilerParams(
            dimension_semantics=("parallel",), vmem_limit_bytes=VMEM_LIMIT),
        name="attn_sample",
    )(proj, proj, proj, cache_k, cache_v, g, new_k, new_v)


def _ssd_kernel(xbc_ref, z_ref, gate_ref, dtr_ref, h0_ref, cprev_ref, sprev_ref,
                cw_ref, cb_ref, alog_ref, dtb_ref, dsk_ref, sn_ref, scw_ref, scn_ref,
                hall_ref, call_ref, sall_ref,
                y_ref, hout_ref, cnew_ref, snew_ref,
                ext_ref, ext2_ref, h_ref, *, chunk, group):
    c = pl.program_id(1)
    nc = pl.num_programs(1)

    @pl.when(c == 0)
    def _init():
        h_ref[...] = h0_ref[...]
        ext_ref[:, 0:8, :] = cprev_ref[...]
        ext2_ref[:, 0:8, :] = sprev_ref[...]

    for i in range(group):
        _ssd_one(xbc_ref.at[i], z_ref.at[i], gate_ref.at[i], dtr_ref.at[i],
                 cw_ref, cb_ref, alog_ref, dtb_ref, dsk_ref, sn_ref, scw_ref, scn_ref,
                 y_ref.at[i], ext_ref.at[i], ext2_ref.at[i], h_ref.at[i], chunk=chunk)

    @pl.when(c == nc - 1)
    def _final():
        hout_ref[...] = h_ref[...]
        cnew_ref[...] = ext_ref[:, chunk:chunk + 8, :]
        snew_ref[...] = ext2_ref[:, chunk:chunk + 8, :]

    ext_ref[:, 0:8, :] = ext_ref[:, chunk:chunk + 8, :]
    ext2_ref[:, 0:8, :] = ext2_ref[:, chunk:chunk + 8, :]


def _ssd_one(*args, **kwargs):
    for _ in _ssd_stages(*args, **kwargs):
        pass


def _ssd_stages(xbc_ref, z_ref, gate_ref, dtr_ref,
                cw_ref, cb_ref, alog_ref, dtb_ref, dsk_ref, sn_ref, scw_ref, scn_ref,
                y_ref, ext_ref, ext2_ref, h_ref, *, chunk):
    C = chunk
    heads_per_group = H_B // SSM_GROUPS
    gw = heads_per_group * HEAD_DIM

    ext_ref[8:8 + C, :] = xbc_ref[...]
    xc_parts = []
    half = CONV_DIM // 2
    for lo in (0, half):
        conv = xbc_ref[:, lo:lo + half] * cw_ref[CONV_W - 1:CONV_W, lo:lo + half]
        for k in range(CONV_W - 1):
            off = 8 - (CONV_W - 1) + k
            conv = conv + ext_ref[off:off + C, lo:lo + half] * cw_ref[k:k + 1, lo:lo + half]
        xc_parts.append(_silu(conv + cb_ref[:, lo:lo + half]))
        yield
    x_s = xc_parts[0]
    bm = xc_parts[1][:, 0:SSM_GROUPS * N_STATE]
    cm = xc_parts[1][:, SSM_GROUPS * N_STATE:2 * SSM_GROUPS * N_STATE]

    gates = gate_ref[...]
    b_gate = gates[:, 0:D_C]
    prod = gates[:, D_C:2 * D_C] * gates[:, 2 * D_C:3 * D_C]
    ext2_ref[8:8 + C, :] = prod
    sconv = prod * scw_ref[SCONV_W - 1:SCONV_W, :]
    for k in range(SCONV_W - 1):
        off = 8 - (SCONV_W - 1) + k
        sconv = sconv + ext2_ref[off:off + C, :] * scw_ref[k:k + 1, :]
    y_ref[:, D_B:D_B + D_C] = _rms(b_gate * sconv, scn_ref[...]).astype(y_ref.dtype)

    row = lax.broadcasted_iota(jnp.int32, (C, C), 0)
    col = lax.broadcasted_iota(jnp.int32, (C, C), 1)
    causal = row >= col
    dt8 = _softplus((dtr_ref[...] + dtb_ref[...]).T[0:H_B, :])
    a8 = jnp.broadcast_to(-jnp.exp(alog_ref[...]), (128, 128)).T[0:H_B, 0:C]
    da = dt8 * a8
    upper = jnp.where(row <= col, 1.0, 0.0).astype(BF16)
    da_hi = da.astype(BF16)
    rem = da - da_hi.astype(F32)
    da_mid = rem.astype(BF16)
    da_lo = (rem - da_mid.astype(F32)).astype(BF16)
    acs8 = (jnp.dot(da_hi, upper, preferred_element_type=F32)
            + jnp.dot(da_mid, upper, preferred_element_type=F32)
            + jnp.dot(da_lo, upper, preferred_element_type=F32))
    last8 = acs8[:, C - 1:C]
    cdec8 = jnp.exp(last8)
    stacked = jnp.concatenate(
        [acs8, jnp.exp(acs8), jnp.exp(last8 - acs8) * dt8, jnp.zeros((128 - 3 * H_B, C), F32)],
        axis=0)
    cols = stacked.T
    lane_lo = lax.broadcasted_iota(jnp.int32, (C, 2 * HEAD_DIM), 1) < HEAD_DIM

    def expand(base):
        return jnp.concatenate(
            [jnp.where(lane_lo, cols[:, base + 2 * p:base + 2 * p + 1],
                       cols[:, base + 2 * p + 1:base + 2 * p + 2]) for p in range(H_B // 2)],
            axis=1)

    e_acs_x = expand(H_B)
    w_end_x = expand(2 * H_B)
    yield

    nt = (((1,), (1,)), ((), ()))
    tn = (((0,), (0,)), ((), ()))
    x_bf = x_s.astype(BF16)
    xw = (x_s * w_end_x)
    y_parts = []
    for g in range(SSM_GROUPS):
        bm_g = bm[:, g * N_STATE:(g + 1) * N_STATE].astype(BF16)
        cm_g = cm[:, g * N_STATE:(g + 1) * N_STATE].astype(BF16)
        scores = lax.dot_general(cm_g, bm_g, nt, preferred_element_type=F32)
        h_g = h_ref[g * heads_per_group:(g + 1) * heads_per_group].reshape(gw, N_STATE)
        y_off = lax.dot_general(cm_g, h_g.astype(BF16), nt, preferred_element_type=F32)
        y_off = y_off * e_acs_x[:, g * gw:(g + 1) * gw]
        diag = []
        for pair in range(heads_per_group // 2):
            mats = []
            for hh in range(2):
                h = g * heads_per_group + 2 * pair + hh
                seg = cols[:, h:h + 1] - acs8[h:h + 1, :]
                decay = jnp.exp(jnp.where(causal, seg, -jnp.inf))
                mats.append((scores * decay * dt8[h:h + 1, :]).astype(BF16))
            lo = (g * heads_per_group + 2 * pair) * HEAD_DIM
            x_pair = x_bf[:, lo:lo + 2 * HEAD_DIM]
            zero = jnp.zeros_like(x_pair)
            x_blk = jnp.concatenate([jnp.where(lane_lo, x_pair, zero),
                                     jnp.where(lane_lo, zero, x_pair)], axis=0)
            diag.append(jnp.dot(jnp.concatenate(mats, axis=1), x_blk,
                                preferred_element_type=F32))
        y_parts.append(jnp.concatenate(diag, axis=1) + y_off)
        st = lax.dot_general(xw[:, g * gw:(g + 1) * gw].astype(BF16), bm_g, tn,
                             preferred_element_type=F32)
        for hh in range(heads_per_group):
            h = g * heads_per_group + hh
            h_ref[h] = h_ref[h] * cdec8[h:h + 1, 0:1] + st[hh * HEAD_DIM:(hh + 1) * HEAD_DIM, :]
        yield
    y = jnp.concatenate(y_parts, axis=1) + dsk_ref[...] * x_s
    y_ref[:, 0:D_B] = _rms(y * _silu(z_ref[...]), sn_ref[...]).astype(y_ref.dtype)
    yield


def _ssd(proj, h0, cprev8, sprev8, wts, outs, layer, batch, nc, chunk, group):
    (cw, cb, alog, dtb, dsk, sn, scw, scn) = wts
    proj3 = proj.reshape(batch, nc * chunk, PROJ_W)

    def tok(width, blk_idx):
        return pl.BlockSpec((group, chunk, width), lambda b, c: (b, c, blk_idx))

    def per_layer(shape):
        return pl.BlockSpec((None,) + shape, lambda b, c: (layer,) + (0,) * len(shape))

    state_spec = pl.BlockSpec((group, H_B, HEAD_DIM, N_STATE), lambda b, c: (b, 0, 0, 0))
    conv_spec = pl.BlockSpec((group, 8, CONV_DIM), lambda b, c: (b, 0, 0))
    sconv_spec = pl.BlockSpec((group, 8, D_C), lambda b, c: (b, 0, 0))

    def stacked(spec):
        return pl.BlockSpec((None,) + tuple(spec.block_shape),
                            lambda b, c: (layer,) + tuple(spec.index_map(b, c)))

    return pl.pallas_call(
        functools.partial(_ssd_kernel, chunk=chunk, group=group),
        grid=(batch // group, nc),
        in_specs=[
            tok(CONV_DIM, COL_XBC // CONV_DIM),
            tok(D_B, COL_Z // D_B),
            tok(3 * D_C, COL_G // (3 * D_C)),
            tok(128, COL_DT // 128),
            stacked(state_spec), stacked(conv_spec), stacked(sconv_spec),
            per_layer((CONV_W, CONV_DIM)), per_layer((1, CONV_DIM)),
            per_layer((1, 128)), per_layer((1, 128)), per_layer((1, D_B)),
            per_layer((1, D_B)), per_layer((SCONV_W, D_C)), per_layer((1, D_C)),
        ] + [pl.BlockSpec(memory_space=pl.ANY)] * 3,
        out_specs=[
            pl.BlockSpec((group, chunk, D_B + D_C), lambda b, c: (b, c, 0)),
            stacked(state_spec), stacked(conv_spec), stacked(sconv_spec),
        ],
        out_shape=[jax.ShapeDtypeStruct((batch, nc * chunk, D_B + D_C), BF16)]
        + [jax.ShapeDtypeStruct(o.shape, F32) for o in outs],
        input_output_aliases={15: 1, 16: 2, 17: 3},
        scratch_shapes=[
            pltpu.VMEM((group, chunk + 8, CONV_DIM), F32),
            pltpu.VMEM((group, chunk + 8, D_C), F32),
            pltpu.VMEM((group, H_B, HEAD_DIM, N_STATE), F32),
        ],
        compiler_params=pltpu.CompilerParams(
            dimension_semantics=("parallel", "arbitrary"), vmem_limit_bytes=VMEM_LIMIT),
        name="ssd_sconv",
    )(proj3, proj3, proj3, proj3, h0, cprev8, sprev8, cw, cb, alog, dtb, dsk, sn, scw, scn, *outs)


def _post_kernel(x_ref, oa0_ref, oa1_ref, ybc_ref, woa_ref, wobc_ref, gpost_ref, gpre_ref,
                 gmlp_ref, wup_ref, wdn_ref, o_ref, acc_ref, *, tf):
    oa = jnp.concatenate([oa0_ref[...], oa1_ref[...]], axis=1).astype(BF16)
    mix = jnp.dot(oa, woa_ref[...], preferred_element_type=F32)
    mix = mix + jnp.dot(ybc_ref[...], wobc_ref[...], preferred_element_type=F32)
    x1 = x_ref[...] + _rms(mix, gpost_ref[...])
    h = _rms(x1, gpre_ref[...]).astype(BF16)
    for j in range(D_FF // tf):
        u = jnp.dot(h, wup_ref[:, j * tf:(j + 1) * tf], preferred_element_type=F32)
        u = jnp.square(jnp.maximum(u, 0.0)).astype(BF16)
        f = jnp.dot(u, wdn_ref[j * tf:(j + 1) * tf, :], preferred_element_type=F32)
        if j == 0:
            acc_ref[...] = f
        else:
            acc_ref[...] += f
    o_ref[...] = x1 + _rms(acc_ref[...], gmlp_ref[...])


def _post(x2d, oa0, oa1, ybc, w_out_a, w_out_bc, gpost, gpre, gmlp, w_up, w_dn, layer, tm, tf):
    m = x2d.shape[0]

    def resident(shape):
        return pl.BlockSpec((None,) + shape, lambda i: (layer, 0, 0),
                            pipeline_mode=pl.Buffered(1))

    return pl.pallas_call(
        functools.partial(_post_kernel, tf=tf),
        grid=(m // tm,),
        in_specs=[
            pl.BlockSpec((tm, D_MODEL), lambda i: (i, 0)),
            pl.BlockSpec((tm, 128), lambda i: (i, 0)),
            pl.BlockSpec((tm, 128), lambda i: (i, 0)),
            pl.BlockSpec((tm, D_B + D_C), lambda i: (i, 0)),
            resident((D_A, D_MODEL)), resident((D_B + D_C, D_MODEL)),
            resident((1, D_MODEL)), resident((1, D_MODEL)), resident((1, D_MODEL)),
            resident((D_MODEL, D_FF)), resident((D_FF, D_MODEL)),
        ],
        out_specs=pl.BlockSpec((tm, D_MODEL), lambda i: (i, 0)),
        out_shape=jax.ShapeDtypeStruct((m, D_MODEL), F32),
        scratch_shapes=[pltpu.VMEM((tm, D_MODEL), F32)],
        compiler_params=pltpu.CompilerParams(
            dimension_semantics=("parallel",), vmem_limit_bytes=VMEM_LIMIT),
        name="post",
    )(x2d, oa0, oa1, ybc, w_out_a, w_out_bc, gpost, gpre, gmlp, w_up, w_dn)


def _fused_kernel(x_ref, oa0_ref, oa1_ref, woa_ref, wobc_ref, gpost_ref, gpre_ref, gmlp_ref,
                  wup_ref, wdn_ref,
                  xbc_ref, z_ref, gate_ref, dtr_ref,
                  cw_ref, cb_ref, alog_ref, dtb_ref, dsk_ref, sn_ref, scw_ref, scn_ref,
                  hall_ref, call_ref, sall_ref,
                  o_ref, hout_ref, cnew_ref, snew_ref,
                  acc_ref, ybc_ref, ext_ref, ext2_ref, h_ref, *, tf, tm, tiles_per_seq, ntiles):
    i = pl.program_id(0)
    slot = i % 2
    nchunks = tm // SSD_CHUNK

    @pl.when(i == 0)
    def _first():
        ybc_ref[...] = jnp.zeros(ybc_ref.shape, ybc_ref.dtype)

    @pl.when(i % tiles_per_seq == 0)
    def _new_sequence():
        h_ref[...] = jnp.zeros(h_ref.shape, F32)
        ext_ref[0:8, :] = jnp.zeros((8, CONV_DIM), F32)
        ext2_ref[0:8, :] = jnp.zeros((8, D_C), F32)

    def ssd_chunk(c):
        rows = slice(c * SSD_CHUNK, (c + 1) * SSD_CHUNK)
        yield from _ssd_stages(
            xbc_ref.at[rows], z_ref.at[rows], gate_ref.at[rows], dtr_ref.at[rows],
            cw_ref, cb_ref, alog_ref, dtb_ref, dsk_ref, sn_ref, scw_ref, scn_ref,
            ybc_ref.at[slot, rows], ext_ref, ext2_ref, h_ref, chunk=SSD_CHUNK)
        ext_ref[0:8, :] = ext_ref[SSD_CHUNK:SSD_CHUNK + 8, :]
        ext2_ref[0:8, :] = ext2_ref[SSD_CHUNK:SSD_CHUNK + 8, :]

    def ssd_tile():
        for c in range(nchunks):
            yield from ssd_chunk(c)

    stages = ssd_tile()
    nff = D_FF // tf
    per_ff = -(-(nchunks * SSD_STAGES) // (nff + 1))

    def advance():
        for _ in range(per_ff):
            next(stages, None)

    oa = jnp.concatenate([oa0_ref[...], oa1_ref[...]], axis=1).astype(BF16)
    mix = jnp.dot(oa, woa_ref[...], preferred_element_type=F32)
    mix = mix + jnp.dot(ybc_ref[1 - slot], wobc_ref[...], preferred_element_type=F32)
    x1 = x_ref[...] + _rms(mix, gpost_ref[...])
    h = _rms(x1, gpre_ref[...]).astype(BF16)
    advance()
    for jj in range(nff // UP_GROUP):
        uu = jnp.dot(h, wup_ref[:, UP_GROUP * jj * tf:UP_GROUP * (jj + 1) * tf],
                     preferred_element_type=F32)
        uu = jnp.square(jnp.maximum(uu, 0.0)).astype(BF16)
        for k in range(UP_GROUP):
            j = UP_GROUP * jj + k
            f = jnp.dot(uu[:, k * tf:(k + 1) * tf], wdn_ref[j * tf:(j + 1) * tf, :],
                        preferred_element_type=F32)
            if j == 0:
                acc_ref[...] = f
            else:
                acc_ref[...] += f
            advance()
    for _ in stages:
        pass
    o_ref[...] = x1 + _rms(acc_ref[...], gmlp_ref[...])

    @pl.when((i % tiles_per_seq == tiles_per_seq - 1) & (i < ntiles))
    def _sequence_done():
        hout_ref[...] = h_ref[...]
        cnew_ref[...] = ext_ref[0:8, :]
        snew_ref[...] = ext2_ref[0:8, :]


def _fused(x2d, oa0, oa1, proj, w_out_a, w_out_bc, gpost, gpre, gmlp, w_up, w_dn, wts, outs,
           layer, batch, tm, tf):
    m = x2d.shape[0]
    (cw, cb, alog, dtb, dsk, sn, scw, scn) = wts
    ntiles = m // tm
    tps = ntiles // batch

    def resident(shape):
        return pl.BlockSpec((None,) + shape, lambda i: (layer,) + (0,) * len(shape),
                            pipeline_mode=pl.Buffered(1))

    def cur(width, blk_idx):
        return pl.BlockSpec((tm, width), lambda i: (jnp.minimum(i, ntiles - 1), blk_idx))

    def prev(width):
        return pl.BlockSpec((tm, width), lambda i: (jnp.maximum(i - 1, 0), 0))

    def per_seq(shape):
        return pl.BlockSpec((None, None) + shape,
                            lambda i: (layer, jnp.minimum(i, ntiles - 1) // tps) + (0,) * len(shape))

    return pl.pallas_call(
        functools.partial(_fused_kernel, tf=tf, tm=tm, tiles_per_seq=tps, ntiles=ntiles),
        grid=(ntiles + 1,),
        in_specs=[
            prev(D_MODEL), prev(128), prev(128),
            resident((D_A, D_MODEL)), resident((D_B + D_C, D_MODEL)),
            resident((1, D_MODEL)), resident((1, D_MODEL)), resident((1, D_MODEL)),
            resident((D_MODEL, D_FF)), resident((D_FF, D_MODEL)),
            cur(CONV_DIM, COL_XBC // CONV_DIM), cur(D_B, COL_Z // D_B),
            cur(3 * D_C, COL_G // (3 * D_C)), cur(128, COL_DT // 128),
            resident((CONV_W, CONV_DIM)), resident((1, CONV_DIM)),
            resident((1, 128)), resident((1, 128)), resident((1, D_B)),
            resident((1, D_B)), resident((SCONV_W, D_C)), resident((1, D_C)),
        ] + [pl.BlockSpec(memory_space=pl.ANY)] * 3,
        out_specs=[prev(D_MODEL), per_seq((H_B, HEAD_DIM, N_STATE)), per_seq((8, CONV_DIM)),
                   per_seq((8, D_C))],
        out_shape=[jax.ShapeDtypeStruct((m, D_MODEL), F32)]
        + [jax.ShapeDtypeStruct(o.shape, F32) for o in outs],
        input_output_aliases={22: 1, 23: 2, 24: 3},
        scratch_shapes=[pltpu.VMEM((tm, D_MODEL), F32),
                        pltpu.VMEM((2, tm, D_B + D_C), BF16),
                        pltpu.VMEM((SSD_CHUNK + 8, CONV_DIM), F32),
                        pltpu.VMEM((SSD_CHUNK + 8, D_C), F32),
                        pltpu.VMEM((H_B, HEAD_DIM, N_STATE), F32)],
        compiler_params=pltpu.CompilerParams(
            dimension_semantics=("arbitrary",), vmem_limit_bytes=VMEM_LIMIT),
        name="ssd_post",
    )(x2d, oa0, oa1, w_out_a, w_out_bc, gpost, gpre, gmlp, w_up, w_dn,
      proj, proj, proj, proj, cw, cb, alog, dtb, dsk, sn, scw, scn, *outs)


def _pack_w_in_kernel(wt_ref, o_ref, *, cols):
    piece = 128
    moves = []
    for name, dst in (("xbc", COL_XBC), ("z", COL_Z), ("q", COL_Q), ("k", COL_Q + D_A),
                      ("v", COL_Q + 2 * D_A), ("b", COL_G), ("c", COL_G + D_C),
                      ("u", COL_G + 2 * D_C)):
        lo, hi = cols[name]
        moves += [(lo + j, dst + j) for j in range(0, hi - lo, piece)]
    for src, dst in moves:
        o_ref[:, dst:dst + piece] = wt_ref[src:src + piece, :].T.astype(BF16)
    lo = cols["dt"][0]
    blk = wt_ref[lo:lo + piece, :].T
    lane = lax.broadcasted_iota(jnp.int32, blk.shape, 1)
    o_ref[:, COL_DT:COL_DT + piece] = jnp.where(lane < H_B, blk, 0.0).astype(BF16)


def _pack_w_in(w_in, cols):
    depth, k, n = w_in.shape
    w_t = jnp.transpose(w_in, (0, 2, 1))
    return pl.pallas_call(
        functools.partial(_pack_w_in_kernel, cols=cols),
        grid=(depth,),
        in_specs=[pl.BlockSpec((None, n, k), lambda l: (l, 0, 0))],
        out_specs=pl.BlockSpec((None, k, PROJ_W), lambda l: (l, 0, 0)),
        out_shape=jax.ShapeDtypeStruct((depth, k, PROJ_W), BF16),
        compiler_params=pltpu.CompilerParams(
            dimension_semantics=("parallel",), vmem_limit_bytes=VMEM_LIMIT),
        name="pack_w_in",
    )(w_t)


def _prep_weights(norm_mix_pre, norm_mix_post, norm_mlp_pre, norm_mlp_post, w_in, w_out,
                  attn_norm, ssm_conv_w, ssm_conv_b, ssm_a_log, ssm_dt_bias, ssm_d, ssm_norm,
                  sconv_w, sconv_norm, w_mlp_up, w_mlp_down):
    depth = w_in.shape[0]
    o = 0
    cols = {}
    for name, width in (("q", D_A), ("k", D_A), ("v", D_A), ("z", D_B), ("xbc", CONV_DIM),
                        ("dt", H_B), ("b", D_C), ("c", D_C), ("u", D_C)):
        cols[name] = (o, o + width)
        o += width

    w_in_r = _pack_w_in(w_in, cols)

    def lane_pad(v):
        return jnp.pad(v.astype(F32), ((0, 0), (0, 128 - H_B)))[:, None, :]

    def row(v):
        return v.astype(F32)[:, None, :]

    return dict(
        g_mix_pre=row(norm_mix_pre), g_mix_post=row(norm_mix_post),
        g_mlp_pre=row(norm_mlp_pre), g_mlp_post=row(norm_mlp_post),
        w_in=w_in_r,
        w_out_a=w_out[:, 0:D_A, :].astype(BF16),
        w_out_bc=w_out[:, D_A:, :].astype(BF16),
        attn_norm=row(attn_norm),
        ssd=(ssm_conv_w.astype(F32), row(ssm_conv_b), lane_pad(ssm_a_log),
             lane_pad(ssm_dt_bias), row(jnp.repeat(ssm_d, HEAD_DIM, axis=1)),
             row(ssm_norm), sconv_w.astype(F32), row(sconv_norm)),
        w_up=w_mlp_up.astype(BF16), w_dn=w_mlp_down.astype(BF16),
    )


def _run_trunk(x, states, w, prompt):
    batch, seq, _ = x.shape
    depth = w["w_in"].shape[0]
    m = batch * seq
    x2d = x.reshape(m, D_MODEL)
    tm = 512 if m % 512 == 0 else m
    if prompt:
        wb = min(WIN_MAX, seq)
    else:
        cache_k, cache_v, st_ssm, st_conv, st_sconv = states
        wb = cache_k.shape[2]
        cache_k = jnp.transpose(cache_k, (0, 1, 3, 4, 2)).reshape(depth, batch, D_A, wb)
        cache_v = jnp.transpose(cache_v, (0, 1, 3, 4, 2)).reshape(depth, batch, D_A, wb)
        h0 = st_ssm.astype(F32)
        cprev8 = jnp.pad(st_conv.astype(F32), ((0, 0), (0, 0), (8 - (CONV_W - 1), 0), (0, 0)))
        sprev8 = jnp.pad(st_sconv.astype(F32), ((0, 0), (0, 0), (8 - (SCONV_W - 1), 0), (0, 0)))
    new_k = lax.empty((depth, batch, D_A, wb), F32)
    new_v = lax.empty((depth, batch, D_A, wb), F32)
    st_out = (lax.empty((depth, batch, H_B, HEAD_DIM, N_STATE), F32),
              lax.empty((depth, batch, 8, CONV_DIM), F32),
              lax.empty((depth, batch, 8, D_C), F32))
    for l in range(depth):
        if prompt:
            proj, qkv_perm, new_k, new_v = _in_proj(x2d, w["g_mix_pre"], w["w_in"], l,
                                                    IN_PROJ_TM if seq % IN_PROJ_TM == 0 else tm,
                                                    seq, wb, new_k, new_v)
            oa0, oa1 = _attn_prompt(qkv_perm, w["attn_norm"], l, batch, seq)
            x2d, *st_out = _fused(x2d, oa0, oa1, proj, w["w_out_a"], w["w_out_bc"],
                                  w["g_mix_post"], w["g_mlp_pre"], w["g_mlp_post"],
                                  w["w_up"], w["w_dn"], w["ssd"], st_out, l, batch, tm, 512)
        else:
            (proj,) = _in_proj(x2d, w["g_mix_pre"], w["w_in"], l, tm)
            oa0, oa1, new_k, new_v = _attn_sample(
                proj, cache_k, cache_v, w["attn_norm"], new_k, new_v, l, batch, seq, wb,
                ATTN_SAMPLE_GROUP if batch % ATTN_SAMPLE_GROUP == 0 else 1)
            ybc, *st_out = _ssd(proj, h0, cprev8, sprev8, w["ssd"], st_out, l, batch, 1, seq,
                                SSD_GROUP if batch % SSD_GROUP == 0 else 1)
            ybc = ybc.reshape(m, D_B + D_C)
            x2d = _post(x2d, oa0, oa1, ybc, w["w_out_a"], w["w_out_bc"], w["g_mix_post"],
                        w["g_mlp_pre"], w["g_mlp_post"], w["w_up"], w["w_dn"], l, tm, 1024)
    h_all, c_all, s_all = st_out
    outs = (jnp.transpose(new_k.reshape(depth, batch, H_A, HEAD_DIM, wb), (0, 1, 4, 2, 3)),
            jnp.transpose(new_v.reshape(depth, batch, H_A, HEAD_DIM, wb), (0, 1, 4, 2, 3)),
            h_all, c_all[:, :, 8 - (CONV_W - 1):, :], s_all[:, :, 8 - (SCONV_W - 1):, :])
    return x2d.reshape(batch, seq, D_MODEL), outs


def kernel(x_prompt, x_sample, cache_attn_k, cache_attn_v, state_ssm, state_ssm_conv, state_sconv, norm_mix_pre, norm_mix_post, norm_mlp_pre, norm_mlp_post, w_in, w_out, attn_norm, ssm_conv_w, ssm_conv_b, ssm_a_log, ssm_dt_bias, ssm_d, ssm_norm, sconv_w, sconv_norm, w_mlp_up, w_mlp_down):
    w = _prep_weights(norm_mix_pre, norm_mix_post, norm_mlp_pre, norm_mlp_post, w_in, w_out,
                      attn_norm, ssm_conv_w, ssm_conv_b, ssm_a_log, ssm_dt_bias, ssm_d,
                      ssm_norm, sconv_w, sconv_norm, w_mlp_up, w_mlp_down)
    y_prompt, (p_k, p_v, p_ssm, p_conv, p_sconv) = _run_trunk(x_prompt, None, w, True)
    states = (cache_attn_k, cache_attn_v, state_ssm, state_ssm_conv, state_sconv)
    y_sample, (s_k, s_v, s_ssm, s_conv, s_sconv) = _run_trunk(x_sample, states, w, False)
    return (y_prompt, y_sample, p_k, p_v, p_ssm, p_conv, p_sconv,
            s_k, s_v, s_ssm, s_conv, s_sconv)
```

```python
import functools

import jax
import jax.numpy as jnp
from jax import lax
from jax.experimental import pallas as pl
from jax.experimental.pallas import tpu as pltpu

F32 = jnp.float32
BF16 = jnp.bfloat16

HEAD_DIM = 64
D_MODEL = 1024
D_A = 256
H_A = 4
D_B = 512
H_B = 8
SSM_GROUPS = 2
N_STATE = 128
CONV_W = 4
CONV_DIM = D_B + 2 * SSM_GROUPS * N_STATE
D_C = 256
SCONV_W = 3
D_FF = 4 * D_MODEL
DILATED_CONFIGS = ((128, 1), (512, 4), (2048, 16))
WIN_MAX = 2048
EPS = 1e-6
LOG2E = 1.4426950408889634
ALIBI_SLOPES = tuple(2.0 ** (-8.0 * (h + 1) / H_A) for h in range(H_A))
ATTN_BLOCK = 128
ATTN_PERM = 16
ATTN_BLOCKS_PER_STEP = 16
ATTN_SAMPLE_GROUP = 2
IN_PROJ_TM = 1024
SSD_CHUNK = 128
SSD_GROUP = 16
UP_GROUP = 2
SSD_STAGES = 6

COL_XBC = 0
COL_Z = 1024
COL_Q = 1536
COL_G = 2304
COL_DT = 3072
PROJ_W = 3200

VMEM_LIMIT = 56 * 1024 * 1024


def _rms(x, g):
    return x * lax.rsqrt(jnp.mean(x * x, axis=-1, keepdims=True) + EPS) * g


def _silu(x):
    return x * (1.0 / (1.0 + jnp.exp(-x)))


def _softplus(x):
    return jnp.maximum(x, 0.0) + jnp.log1p(jnp.exp(-jnp.abs(x)))


def _in_proj_kernel(x_ref, g_ref, w_ref, *rest, permute):
    o_ref = rest[2] if permute else rest[0]
    h = _rms(x_ref[...], g_ref[...]).astype(BF16)
    for lo, hi in ((0, COL_Q), (COL_Q, COL_G), (COL_G, PROJ_W)):
        res = jnp.dot(h, w_ref[:, lo:hi], preferred_element_type=F32)
        o_ref[:, lo:hi] = res
        if permute and lo == COL_Q:
            operm_ref, pkt_ref, pvt_ref = rest[3:]
            tm = x_ref.shape[0]
            pkt_ref[...] = res[:, D_A:2 * D_A].T
            pvt_ref[...] = res[:, 2 * D_A:3 * D_A].T
            operm_ref[...] = jnp.swapaxes(
                res.reshape(tm // ATTN_PERM, ATTN_PERM, 3 * D_A), 0, 1)


def _in_proj(x2d, g, w, layer, tm, seq=None, wb=None, pkt=None, pvt=None):
    m = x2d.shape[0]
    permute = seq is not None
    in_specs = [
        pl.BlockSpec((tm, D_MODEL), lambda i: (i, 0)),
        pl.BlockSpec((None, 1, D_MODEL), lambda i: (layer, 0, 0)),
        pl.BlockSpec((None, D_MODEL, PROJ_W), lambda i: (layer, 0, 0),
                     pipeline_mode=pl.Buffered(1)),
    ]
    args = [x2d, g, w]
    out_specs = [pl.BlockSpec((tm, PROJ_W), lambda i: (i, 0))]
    out_shape = [jax.ShapeDtypeStruct((m, PROJ_W), F32)]
    scratch = []
    aliases = {}
    if permute:
        tiles = seq // tm
        rows = tm // ATTN_PERM
        first = (seq - wb) // tm
        out_specs.append(pl.BlockSpec((None, ATTN_PERM, rows, 3 * D_A),
                                      lambda i: (i // tiles, 0, i % tiles, 0)))
        out_shape.append(jax.ShapeDtypeStruct((m // seq, ATTN_PERM, seq // ATTN_PERM, 3 * D_A), F32))
        win_spec = pl.BlockSpec((None, None, D_A, tm),
                                lambda i: (layer, i // tiles, 0, jnp.maximum(i % tiles - first, 0)))
        out_specs += [win_spec, win_spec]
        out_shape += [jax.ShapeDtypeStruct(pkt.shape, F32)] * 2
        in_specs += [pl.BlockSpec(memory_space=pl.ANY)] * 2
        args += [pkt, pvt]
        aliases = {3: 2, 4: 3}
    return pl.pallas_call(
        functools.partial(_in_proj_kernel, permute=permute),
        grid=(m // tm,),
        in_specs=in_specs,
        out_specs=out_specs,
        out_shape=out_shape,
        input_output_aliases=aliases,
        scratch_shapes=scratch,
        compiler_params=pltpu.CompilerParams(
            dimension_semantics=("arbitrary",), vmem_limit_bytes=VMEM_LIMIT),
        name="in_proj",
    )(*args)


def _attn_prompt_kernel(q_ref, k_ref, v_ref, g_ref, o0_ref, o1_ref,
                        acc_ref, m_ref, l_ref, bias_ref, *, seq):
    blk = ATTN_BLOCK
    nbr = len(DILATED_CONFIGS)
    P = ATTN_PERM
    per_res = seq // P

    def local_to_strided(a, dil):
        nchunk = P // dil
        rows = blk // nchunk
        return nchunk * (a % rows) + a // rows

    shape4 = (H_A * blk, 2 * blk)
    row4 = lax.broadcasted_iota(jnp.int32, shape4, 0)
    ki = lax.broadcasted_iota(jnp.int32, shape4, 1)
    slope = jnp.zeros(shape4, F32)
    for h in range(H_A):
        slope = jnp.where(row4 // blk == h, ALIBI_SLOPES[h], slope)
    for bi, (win, dil) in enumerate(DILATED_CONFIGS):
        assert win // dil == blk and P % dil == 0
        jq = blk + local_to_strided(row4 % blk, dil)
        jk = local_to_strided(ki % blk, dil) + blk * (ki // blk)
        diff = jq - jk
        band = (diff >= 0) & (diff <= blk)
        dist = slope * (-float(dil) * LOG2E * diff.astype(F32))
        bias_ref[bi] = jnp.where(band, dist, -jnp.inf)
        bias_ref[nbr + bi] = jnp.where(band & (ki >= blk), dist, -jnp.inf)

    lane_head = lax.broadcasted_iota(jnp.int32, (blk, D_A), 1) // HEAD_DIM
    lane_lo = lax.broadcasted_iota(jnp.int32, (blk, 128), 1) < HEAD_DIM

    def per_head(col):
        c = [col[h * blk:(h + 1) * blk, :] for h in range(H_A)]
        return jnp.concatenate([jnp.where(lane_lo, c[0], c[1]), jnp.where(lane_lo, c[2], c[3])],
                               axis=1)

    def blocks(r, u, bi, dil):
        nchunk = P // dil
        rows = blk // nchunk
        nb = min(ATTN_BLOCKS_PER_STEP, per_res // rows)
        starts = [pl.multiple_of(jnp.maximum(nb * u - 1 + j, 0) * rows, rows)
                  for j in range(nb + 1)]

        def load(ref, st):
            return jnp.concatenate(
                [ref[dil * c + r, pl.ds(st, rows), :] for c in range(nchunk)], axis=0)

        def store(ref, st, val):
            for c in range(nchunk):
                ref[dil * c + r, pl.ds(st, rows), :] = val[c * rows:(c + 1) * rows, :]

        kb = [load(k_ref, st).astype(BF16) for st in starts]
        vb = [load(v_ref, st).astype(BF16) for st in starts]
        results = []
        for j in range(nb):
            q = (load(q_ref, starts[1 + j]) * (HEAD_DIM ** -0.5 * LOG2E)).astype(BF16)
            q4 = jnp.concatenate(
                [jnp.where(lane_head == h, q, jnp.zeros_like(q)) for h in range(H_A)], axis=0)
            kk = jnp.concatenate([kb[j], kb[j + 1]], axis=0)
            vv = jnp.concatenate([vb[j], vb[j + 1]], axis=0)
            s = lax.dot_general(q4, kk, (((1,), (1,)), ((), ())), preferred_element_type=F32)
            if j == 0:
                s = s + bias_ref[jnp.where(u == 0, nbr + bi, bi)]
            else:
                s = s + bias_ref[bi]
            m4 = jnp.max(s, axis=-1, keepdims=True)
            p = jnp.exp2(s - m4)
            l4 = jnp.sum(p, axis=-1, keepdims=True)
            pb = p.astype(BF16)
            o = [jnp.dot(pb[h * blk:(h + 1) * blk, :],
                         vv[:, (h // 2) * 128:(h // 2 + 1) * 128],
                         preferred_element_type=F32) for h in range(H_A)]
            acc_b = jnp.concatenate([jnp.where(lane_lo, o[0], o[1]),
                                     jnp.where(lane_lo, o[2], o[3])], axis=1)
            m_b = per_head(m4)
            l_b = per_head(l4)
            if bi > 0:
                st = starts[1 + j]
                m_old = load(m_ref, st)
                m_new = jnp.maximum(m_old, m_b)
                a_old = jnp.exp2(m_old - m_new)
                a_b = jnp.exp2(m_b - m_new)
                acc_b = load(acc_ref, st) * a_old + acc_b * a_b
                l_b = load(l_ref, st) * a_old + l_b * a_b
                m_b = m_new
            results.append((acc_b, m_b, l_b))
        for j in range(nb):
            if bi < nbr - 1:
                store(acc_ref, starts[1 + j], results[j][0])
                store(m_ref, starts[1 + j], results[j][1])
                store(l_ref, starts[1 + j], results[j][2])
            else:
                assert dil == P and nchunk == 1
                o = _rms(results[j][0] / results[j][2], g_ref[...])
                out_rows = pl.ds(r + P * starts[1 + j], blk, stride=P)
                o0_ref[out_rows, :] = o[:, 0:128]
                o1_ref[out_rows, :] = o[:, 128:256]

    for bi, (win, dil) in enumerate(DILATED_CONFIGS):
        nblk = per_res // (blk // (P // dil))
        nb = min(ATTN_BLOCKS_PER_STEP, nblk)
        nstep = nblk // nb
        nres = ATTN_BLOCKS_PER_STEP // nb
        assert dil % nres == 0

        def body(it, carry, bi=bi, dil=dil, nstep=nstep, nres=nres):
            rg = it // nstep
            u = it - rg * nstep
            for rr in range(nres):
                blocks(rg * nres + rr, u, bi, dil)
            return carry

        lax.fori_loop(0, (dil // nres) * nstep, body, 0)


def _attn_prompt(qkv_perm, g, layer, batch, seq):
    per_res = seq // ATTN_PERM
    qkv_specs = [pl.BlockSpec((None, ATTN_PERM, per_res, D_A),
                              functools.partial(lambda b, j: (b, 0, 0, j), j=j)) for j in range(3)]
    half_spec = pl.BlockSpec((seq, 128), lambda b: (b, 0))
    return pl.pallas_call(
        functools.partial(_attn_prompt_kernel, seq=seq),
        grid=(batch,),
        in_specs=qkv_specs + [pl.BlockSpec((None, 1, D_A), lambda b: (layer, 0, 0))],
        out_specs=[half_spec, half_spec],
        out_shape=[jax.ShapeDtypeStruct((batch * seq, 128), F32)] * 2,
        scratch_shapes=[pltpu.VMEM((ATTN_PERM, per_res, D_A), F32)] * 3
        + [pltpu.VMEM((2 * len(DILATED_CONFIGS), H_A * ATTN_BLOCK, 2 * ATTN_BLOCK), F32)],
        compiler_params=pltpu.CompilerParams(
            dimension_semantics=("parallel",), vmem_limit_bytes=VMEM_LIMIT),
        name="attn_prompt",
    )(qkv_perm, qkv_perm, qkv_perm, g)


def _attn_sample_kernel(q_ref, k_ref, v_ref, ckt_ref, cvt_ref, g_ref, nk_in_ref, nv_in_ref,
                        o0_ref, o1_ref, nkt_ref, nvt_ref, kpad_ref, vpad_ref, *, wb, s_len, group):
    pad = 128
    rows = H_A * s_len
    ncol = wb + pad
    row_i = lax.broadcasted_iota(jnp.int32, (rows, ncol), 0)
    col_i = lax.broadcasted_iota(jnp.int32, (rows, ncol), 1)
    s_i = row_i % s_len
    pos = jnp.where(col_i < wb, col_i, col_i + s_len - pad)
    d = wb + s_i - pos
    real = (col_i < wb) | (col_i >= ncol - s_len)
    mult = jnp.zeros((rows, ncol), F32)
    for win, dil in DILATED_CONFIGS:
        hit = real & (d >= 0) & (d <= win) & ((d % dil) == 0)
        mult = mult + jnp.where(hit, 1.0, 0.0)
    slope = jnp.zeros((rows, ncol), F32)
    for h in range(H_A):
        slope = jnp.where(row_i // s_len == h, ALIBI_SLOPES[h], slope)
    bias = jnp.where(mult > 0.0, -slope * d.astype(F32), -jnp.inf)
    tail = lax.broadcasted_iota(jnp.int32, (D_A, pad), 1) >= pad - s_len
    lane_head = lax.broadcasted_iota(jnp.int32, (s_len, D_A), 1) // HEAD_DIM
    nt = (((1,), (1,)), ((), ()))

    for i in range(group):
        tok = slice(i * s_len, (i + 1) * s_len)
        k_new = k_ref[tok, :]
        v_new = v_ref[tok, :]
        kpad_ref[i] = jnp.zeros(kpad_ref.shape[1:], F32)
        vpad_ref[i] = jnp.zeros(vpad_ref.shape[1:], F32)
        kpad_ref[i, pad - s_len:pad, :] = k_new
        vpad_ref[i, pad - s_len:pad, :] = v_new
        kpad = kpad_ref[i]
        vpad = vpad_ref[i]
        kpad_t = kpad.T
        vpad_t = vpad.T

        for src_ref, dst_ref, new_t in ((ckt_ref, nkt_ref, kpad_t), (cvt_ref, nvt_ref, vpad_t)):
            rolled = pltpu.roll(src_ref[i], wb - s_len, axis=1)
            dst_ref[i, :, 0:wb - pad] = rolled[:, 0:wb - pad]
            dst_ref[i, :, wb - pad:wb] = jnp.where(tail, new_t, rolled[:, wb - pad:wb])

        q = q_ref[tok, :] * (HEAD_DIM ** -0.5)
        qh = jnp.concatenate([jnp.where(lane_head == h, q, 0.0) for h in range(H_A)],
                             axis=0).astype(BF16)
        s1 = jnp.dot(qh, ckt_ref[i].astype(BF16), preferred_element_type=F32)
        s2 = jnp.dot(qh, kpad_t.astype(BF16), preferred_element_type=F32)
        s = jnp.concatenate([s1, s2], axis=1) + bias
        m = jnp.max(s, axis=-1, keepdims=True)
        p = jnp.exp(s - m) * mult
        l = jnp.sum(p, axis=-1, keepdims=True)
        pb = p.astype(BF16)
        o = lax.dot_general(pb[:, 0:wb], cvt_ref[i].astype(BF16), nt, preferred_element_type=F32)
        o = o + jnp.dot(pb[:, wb:ncol], vpad.astype(BF16), preferred_element_type=F32)
        o = o / l
        out = jnp.zeros((s_len, D_A), F32)
        for h in range(H_A):
            out = jnp.where(lane_head == h, o[h * s_len:(h + 1) * s_len, :], out)
        out = _rms(out, g_ref[...])
        o0_ref[tok, :] = out[:, 0:128]
        o1_ref[tok, :] = out[:, 128:256]


def _attn_sample(proj, cache_k, cache_v, g, new_k, new_v, layer, batch, s_len, wb, group):
    depth = cache_k.shape[0]
    qb = COL_Q // D_A
    cache_spec = pl.BlockSpec((None, group, D_A, wb), lambda b: (layer, b, 0, 0))
    rows = group * s_len
    return pl.pallas_call(
        functools.partial(_attn_sample_kernel, wb=wb, s_len=s_len, group=group),
        grid=(batch // group,),
        in_specs=[
            pl.BlockSpec((rows, D_A), lambda b: (b, qb)),
            pl.BlockSpec((rows, D_A), lambda b: (b, qb + 1)),
            pl.BlockSpec((rows, D_A), lambda b: (b, qb + 2)),
            cache_spec, cache_spec,
            pl.BlockSpec((None, 1, D_A), lambda b: (layer, 0, 0)),
            pl.BlockSpec(memory_space=pl.ANY), pl.BlockSpec(memory_space=pl.ANY),
        ],
        out_specs=[pl.BlockSpec((rows, 128), lambda b: (b, 0))] * 2 + [cache_spec, cache_spec],
        out_shape=[jax.ShapeDtypeStruct((batch * s_len, 128), F32)] * 2 + [
                   jax.ShapeDtypeStruct((depth, batch, D_A, wb), F32)] * 2,
        input_output_aliases={6: 2, 7: 3},
        scratch_shapes=[pltpu.VMEM((group, 128, D_A), F32)] * 2,
        compiler_params=pltpu.CompilerParams(
            dimension_semantics=("parallel",), vmem_limit_bytes=VMEM_LIMIT),
        name="attn_sample",
    )(proj, proj, proj, cache_k, cache_v, g, new_k, new_v)


def _ssd_kernel(xbc_ref, z_ref, gate_ref, dtr_ref, h0_ref, cprev_ref, sprev_ref,
                cw_ref, cb_ref, alog_ref, dtb_ref, dsk_ref, sn_ref, scw_ref, scn_ref,
                hall_ref, call_ref, sall_ref,
                y_ref, hout_ref, cnew_ref, snew_ref,
                ext_ref, ext2_ref, h_ref, *, chunk, group):
    c = pl.program_id(1)
    nc = pl.num_programs(1)

    @pl.when(c == 0)
    def _init():
        h_ref[...] = h0_ref[...]
        ext_ref[:, 0:8, :] = cprev_ref[...]
        ext2_ref[:, 0:8, :] = sprev_ref[...]

    gens = [_ssd_stages(xbc_ref.at[i], z_ref.at[i], gate_ref.at[i], dtr_ref.at[i],
                        cw_ref, cb_ref, alog_ref, dtb_ref, dsk_ref, sn_ref, scw_ref, scn_ref,
                        y_ref.at[i], ext_ref.at[i], ext2_ref.at[i], h_ref.at[i], chunk=chunk)
            for i in range(group)]
    for _ in range(SSD_STAGES):
        for gen in gens:
            next(gen)

    @pl.when(c == nc - 1)
    def _final():
        hout_ref[...] = h_ref[...]
        cnew_ref[...] = ext_ref[:, chunk:chunk + 8, :]
        snew_ref[...] = ext2_ref[:, chunk:chunk + 8, :]

    ext_ref[:, 0:8, :] = ext_ref[:, chunk:chunk + 8, :]
    ext2_ref[:, 0:8, :] = ext2_ref[:, chunk:chunk + 8, :]


def _ssd_stages(xbc_ref, z_ref, gate_ref, dtr_ref,
                cw_ref, cb_ref, alog_ref, dtb_ref, dsk_ref, sn_ref, scw_ref, scn_ref,
                y_ref, ext_ref, ext2_ref, h_ref, *, chunk):
    C = chunk
    heads_per_group = H_B // SSM_GROUPS
    gw = heads_per_group * HEAD_DIM

    ext_ref[8:8 + C, :] = xbc_ref[...]
    xc_parts = []
    half = CONV_DIM // 2
    for lo in (0, half):
        conv = xbc_ref[:, lo:lo + half] * cw_ref[CONV_W - 1:CONV_W, lo:lo + half]
        for k in range(CONV_W - 1):
            off = 8 - (CONV_W - 1) + k
            conv = conv + ext_ref[off:off + C, lo:lo + half] * cw_ref[k:k + 1, lo:lo + half]
        xc_parts.append(_silu(conv + cb_ref[:, lo:lo + half]))
        yield
    x_s = xc_parts[0]
    bm = xc_parts[1][:, 0:SSM_GROUPS * N_STATE]
    cm = xc_parts[1][:, SSM_GROUPS * N_STATE:2 * SSM_GROUPS * N_STATE]

    gates = gate_ref[...]
    b_gate = gates[:, 0:D_C]
    prod = gates[:, D_C:2 * D_C] * gates[:, 2 * D_C:3 * D_C]
    ext2_ref[8:8 + C, :] = prod
    sconv = prod * scw_ref[SCONV_W - 1:SCONV_W, :]
    for k in range(SCONV_W - 1):
        off = 8 - (SCONV_W - 1) + k
        sconv = sconv + ext2_ref[off:off + C, :] * scw_ref[k:k + 1, :]
    y_ref[:, D_B:D_B + D_C] = _rms(b_gate * sconv, scn_ref[...]).astype(y_ref.dtype)

    row = lax.broadcasted_iota(jnp.int32, (C, C), 0)
    col = lax.broadcasted_iota(jnp.int32, (C, C), 1)
    causal = row >= col
    dt8 = _softplus((dtr_ref[...] + dtb_ref[...]).T[0:H_B, :])
    a8 = jnp.broadcast_to(-jnp.exp(alog_ref[...]), (128, 128)).T[0:H_B, 0:C]
    da = dt8 * a8
    upper = jnp.where(row <= col, 1.0, 0.0).astype(BF16)
    da_hi = da.astype(BF16)
    rem = da - da_hi.astype(F32)
    da_mid = rem.astype(BF16)
    da_lo = (rem - da_mid.astype(F32)).astype(BF16)
    acs8 = (jnp.dot(da_hi, upper, preferred_element_type=F32)
            + jnp.dot(da_mid, upper, preferred_element_type=F32)
            + jnp.dot(da_lo, upper, preferred_element_type=F32))
    last8 = acs8[:, C - 1:C]
    cdec8 = jnp.exp(last8)
    stacked = jnp.concatenate(
        [acs8, jnp.exp(acs8), jnp.exp(last8 - acs8) * dt8, jnp.zeros((128 - 3 * H_B, C), F32)],
        axis=0)
    cols = stacked.T
    lane_lo = lax.broadcasted_iota(jnp.int32, (C, 2 * HEAD_DIM), 1) < HEAD_DIM

    def expand(base):
        return jnp.concatenate(
            [jnp.where(lane_lo, cols[:, base + 2 * p:base + 2 * p + 1],
                       cols[:, base + 2 * p + 1:base + 2 * p + 2]) for p in range(H_B // 2)],
            axis=1)

    e_acs_x = expand(H_B)
    w_end_x = expand(2 * H_B)
    yield

    nt = (((1,), (1,)), ((), ()))
    tn = (((0,), (0,)), ((), ()))
    x_bf = x_s.astype(BF16)
    xw = (x_s * w_end_x)
    y_parts = []
    for g in range(SSM_GROUPS):
        bm_g = bm[:, g * N_STATE:(g + 1) * N_STATE].astype(BF16)
        cm_g = cm[:, g * N_STATE:(g + 1) * N_STATE].astype(BF16)
        scores = lax.dot_general(cm_g, bm_g, nt, preferred_element_type=F32)
        h_g = h_ref[g * heads_per_group:(g + 1) * heads_per_group].reshape(gw, N_STATE)
        y_off = lax.dot_general(cm_g, h_g.astype(BF16), nt, preferred_element_type=F32)
        y_off = y_off * e_acs_x[:, g * gw:(g + 1) * gw]
        diag = []
        for pair in range(heads_per_group // 2):
            mats = []
            for hh in range(2):
                h = g * heads_per_group + 2 * pair + hh
                seg = cols[:, h:h + 1] - acs8[h:h + 1, :]
                decay = jnp.exp(jnp.where(causal, seg, -jnp.inf))
                mats.append((scores * decay * dt8[h:h + 1, :]).astype(BF16))
            lo = (g * heads_per_group + 2 * pair) * HEAD_DIM
            x_pair = x_bf[:, lo:lo + 2 * HEAD_DIM]
            zero = jnp.zeros_like(x_pair)
            x_blk = jnp.concatenate([jnp.where(lane_lo, x_pair, zero),
                                     jnp.where(lane_lo, zero, x_pair)], axis=0)
            diag.append(jnp.dot(jnp.concatenate(mats, axis=1), x_blk,
                                preferred_element_type=F32))
        y_parts.append(jnp.concatenate(diag, axis=1) + y_off)
        st = lax.dot_general(xw[:, g * gw:(g + 1) * gw].astype(BF16), bm_g, tn,
                             preferred_element_type=F32)
        for hh in range(heads_per_group):
            h = g * heads_per_group + hh
            h_ref[h] = h_ref[h] * cdec8[h:h + 1, 0:1] + st[hh * HEAD_DIM:(hh + 1) * HEAD_DIM, :]
        yield
    y = jnp.concatenate(y_parts, axis=1) + dsk_ref[...] * x_s
    y_ref[:, 0:D_B] = _rms(y * _silu(z_ref[...]), sn_ref[...]).astype(y_ref.dtype)
    yield


def _ssd(proj, h0, cprev8, sprev8, wts, outs, layer, batch, nc, chunk, group):
    (cw, cb, alog, dtb, dsk, sn, scw, scn) = wts
    proj3 = proj.reshape(batch, nc * chunk, PROJ_W)

    def tok(width, blk_idx):
        return pl.BlockSpec((group, chunk, width), lambda b, c: (b, c, blk_idx))

    def per_layer(shape):
        return pl.BlockSpec((None,) + shape, lambda b, c: (layer,) + (0,) * len(shape))

    state_spec = pl.BlockSpec((group, H_B, HEAD_DIM, N_STATE), lambda b, c: (b, 0, 0, 0))
    conv_spec = pl.BlockSpec((group, 8, CONV_DIM), lambda b, c: (b, 0, 0))
    sconv_spec = pl.BlockSpec((group, 8, D_C), lambda b, c: (b, 0, 0))

    def stacked(spec):
        return pl.BlockSpec((None,) + tuple(spec.block_shape),
                            lambda b, c: (layer,) + tuple(spec.index_map(b, c)))

    return pl.pallas_call(
        functools.partial(_ssd_kernel, chunk=chunk, group=group),
        grid=(batch // group, nc),
        in_specs=[
            tok(CONV_DIM, COL_XBC // CONV_DIM),
            tok(D_B, COL_Z // D_B),
            tok(3 * D_C, COL_G // (3 * D_C)),
            tok(128, COL_DT // 128),
            stacked(state_spec), stacked(conv_spec), stacked(sconv_spec),
            per_layer((CONV_W, CONV_DIM)), per_layer((1, CONV_DIM)),
            per_layer((1, 128)), per_layer((1, 128)), per_layer((1, D_B)),
            per_layer((1, D_B)), per_layer((SCONV_W, D_C)), per_layer((1, D_C)),
        ] + [pl.BlockSpec(memory_space=pl.ANY)] * 3,
        out_specs=[
            pl.BlockSpec((group, chunk, D_B + D_C), lambda b, c: (b, c, 0)),
            stacked(state_spec), stacked(conv_spec), stacked(sconv_spec),
        ],
        out_shape=[jax.ShapeDtypeStruct((batch, nc * chunk, D_B + D_C), BF16)]
        + [jax.ShapeDtypeStruct(o.shape, F32) for o in outs],
        input_output_aliases={15: 1, 16: 2, 17: 3},
        scratch_shapes=[
            pltpu.VMEM((group, chunk + 8, CONV_DIM), F32),
            pltpu.VMEM((group, chunk + 8, D_C), F32),
            pltpu.VMEM((group, H_B, HEAD_DIM, N_STATE), F32),
        ],
        compiler_params=pltpu.CompilerParams(
            dimension_semantics=("parallel", "arbitrary"), vmem_limit_bytes=VMEM_LIMIT),
        name="ssd_sconv",
    )(proj3, proj3, proj3, proj3, h0, cprev8, sprev8, cw, cb, alog, dtb, dsk, sn, scw, scn, *outs)


def _post_kernel(x_ref, oa0_ref, oa1_ref, ybc_ref, woa_ref, wobc_ref, gpost_ref, gpre_ref,
                 gmlp_ref, wup_ref, wdn_ref, o_ref, acc_ref, *, tf):
    oa = jnp.concatenate([oa0_ref[...], oa1_ref[...]], axis=1).astype(BF16)
    mix = jnp.dot(oa, woa_ref[...], preferred_element_type=F32)
    mix = mix + jnp.dot(ybc_ref[...], wobc_ref[...], preferred_element_type=F32)
    x1 = x_ref[...] + _rms(mix, gpost_ref[...])
    h = _rms(x1, gpre_ref[...]).astype(BF16)
    for j in range(D_FF // tf):
        u = jnp.dot(h, wup_ref[:, j * tf:(j + 1) * tf], preferred_element_type=F32)
        u = jnp.square(jnp.maximum(u, 0.0)).astype(BF16)
        f = jnp.dot(u, wdn_ref[j * tf:(j + 1) * tf, :], preferred_element_type=F32)
        if j == 0:
            acc_ref[...] = f
        else:
            acc_ref[...] += f
    o_ref[...] = x1 + _rms(acc_ref[...], gmlp_ref[...])


def _post(x2d, oa0, oa1, ybc, w_out_a, w_out_bc, gpost, gpre, gmlp, w_up, w_dn, layer, tm, tf):
    m = x2d.shape[0]

    def resident(shape):
        return pl.BlockSpec((None,) + shape, lambda i: (layer, 0, 0),
                            pipeline_mode=pl.Buffered(1))

    return pl.pallas_call(
        functools.partial(_post_kernel, tf=tf),
        grid=(m // tm,),
        in_specs=[
            pl.BlockSpec((tm, D_MODEL), lambda i: (i, 0)),
            pl.BlockSpec((tm, 128), lambda i: (i, 0)),
            pl.BlockSpec((tm, 128), lambda i: (i, 0)),
            pl.BlockSpec((tm, D_B + D_C), lambda i: (i, 0)),
            resident((D_A, D_MODEL)), resident((D_B + D_C, D_MODEL)),
            resident((1, D_MODEL)), resident((1, D_MODEL)), resident((1, D_MODEL)),
            resident((D_MODEL, D_FF)), resident((D_FF, D_MODEL)),
        ],
        out_specs=pl.BlockSpec((tm, D_MODEL), lambda i: (i, 0)),
        out_shape=jax.ShapeDtypeStruct((m, D_MODEL), F32),
        scratch_shapes=[pltpu.VMEM((tm, D_MODEL), F32)],
        compiler_params=pltpu.CompilerParams(
            dimension_semantics=("parallel",), vmem_limit_bytes=VMEM_LIMIT),
        name="post",
    )(x2d, oa0, oa1, ybc, w_out_a, w_out_bc, gpost, gpre, gmlp, w_up, w_dn)


def _fused_kernel(x_ref, oa0_ref, oa1_ref, woa_ref, wobc_ref, gpost_ref, gpre_ref, gmlp_ref,
                  wup_ref, wdn_ref,
                  xbc_ref, z_ref, gate_ref, dtr_ref,
                  cw_ref, cb_ref, alog_ref, dtb_ref, dsk_ref, sn_ref, scw_ref, scn_ref,
                  hall_ref, call_ref, sall_ref,
                  o_ref, hout_ref, cnew_ref, snew_ref,
                  acc_ref, ybc_ref, ext_ref, ext2_ref, h_ref, *, tf, tm, tiles_per_seq, ntiles):
    i = pl.program_id(0)
    slot = i % 2
    nchunks = tm // SSD_CHUNK

    @pl.when(i == 0)
    def _first():
        ybc_ref[...] = jnp.zeros(ybc_ref.shape, ybc_ref.dtype)

    @pl.when(i % tiles_per_seq == 0)
    def _new_sequence():
        h_ref[...] = jnp.zeros(h_ref.shape, F32)
        ext_ref[0:8, :] = jnp.zeros((8, CONV_DIM), F32)
        ext2_ref[0:8, :] = jnp.zeros((8, D_C), F32)

    def ssd_chunk(c):
        rows = slice(c * SSD_CHUNK, (c + 1) * SSD_CHUNK)
        yield from _ssd_stages(
            xbc_ref.at[rows], z_ref.at[rows], gate_ref.at[rows], dtr_ref.at[rows],
            cw_ref, cb_ref, alog_ref, dtb_ref, dsk_ref, sn_ref, scw_ref, scn_ref,
            ybc_ref.at[slot, rows], ext_ref, ext2_ref, h_ref, chunk=SSD_CHUNK)
        ext_ref[0:8, :] = ext_ref[SSD_CHUNK:SSD_CHUNK + 8, :]
        ext2_ref[0:8, :] = ext2_ref[SSD_CHUNK:SSD_CHUNK + 8, :]

    def ssd_tile():
        for c in range(nchunks):
            yield from ssd_chunk(c)

    stages = ssd_tile()
    nff = D_FF // tf
    per_ff = -(-(nchunks * SSD_STAGES) // (nff + 1))

    def advance():
        for _ in range(per_ff):
            next(stages, None)

    oa = jnp.concatenate([oa0_ref[...], oa1_ref[...]], axis=1).astype(BF16)
    mix = jnp.dot(oa, woa_ref[...], preferred_element_type=F32)
    mix = mix + jnp.dot(ybc_ref[1 - slot], wobc_ref[...], preferred_element_type=F32)
    x1 = x_ref[...] + _rms(mix, gpost_ref[...])
    h = _rms(x1, gpre_ref[...]).astype(BF16)
    advance()
    for jj in range(nff // UP_GROUP):
        uu = jnp.dot(h, wup_ref[:, UP_GROUP * jj * tf:UP_GROUP * (jj + 1) * tf],
                     preferred_element_type=F32)
        uu = jnp.square(jnp.maximum(uu, 0.0)).astype(BF16)
        for k in range(UP_GROUP):
            j = UP_GROUP * jj + k
            f = jnp.dot(uu[:, k * tf:(k + 1) * tf], wdn_ref[j * tf:(j + 1) * tf, :],
                        preferred_element_type=F32)
            if j == 0:
                acc_ref[...] = f
            else:
                acc_ref[...] += f
            advance()
    for _ in stages:
        pass
    o_ref[...] = x1 + _rms(acc_ref[...], gmlp_ref[...])

    @pl.when((i % tiles_per_seq == tiles_per_seq - 1) & (i < ntiles))
    def _sequence_done():
        hout_ref[...] = h_ref[...]
        cnew_ref[...] = ext_ref[0:8, :]
        snew_ref[...] = ext2_ref[0:8, :]


def _fused(x2d, oa0, oa1, proj, w_out_a, w_out_bc, gpost, gpre, gmlp, w_up, w_dn, wts, outs,
           layer, batch, tm, tf):
    m = x2d.shape[0]
    (cw, cb, alog, dtb, dsk, sn, scw, scn) = wts
    ntiles = m // tm
    tps = ntiles // batch

    def resident(shape):
        return pl.BlockSpec((None,) + shape, lambda i: (layer,) + (0,) * len(shape),
                            pipeline_mode=pl.Buffered(1))

    def cur(width, blk_idx):
        return pl.BlockSpec((tm, width), lambda i: (jnp.minimum(i, ntiles - 1), blk_idx))

    def prev(width):
        return pl.BlockSpec((tm, width), lambda i: (jnp.maximum(i - 1, 0), 0))

    def per_seq(shape):
        return pl.BlockSpec((None, None) + shape,
                            lambda i: (layer, jnp.minimum(i, ntiles - 1) // tps) + (0,) * len(shape))

    return pl.pallas_call(
        functools.partial(_fused_kernel, tf=tf, tm=tm, tiles_per_seq=tps, ntiles=ntiles),
        grid=(ntiles + 1,),
        in_specs=[
            prev(D_MODEL), prev(128), prev(128),
            resident((D_A, D_MODEL)), resident((D_B + D_C, D_MODEL)),
            resident((1, D_MODEL)), resident((1, D_MODEL)), resident((1, D_MODEL)),
            resident((D_MODEL, D_FF)), resident((D_FF, D_MODEL)),
            cur(CONV_DIM, COL_XBC // CONV_DIM), cur(D_B, COL_Z // D_B),
            cur(3 * D_C, COL_G // (3 * D_C)), cur(128, COL_DT // 128),
            resident((CONV_W, CONV_DIM)), resident((1, CONV_DIM)),
            resident((1, 128)), resident((1, 128)), resident((1, D_B)),
            resident((1, D_B)), resident((SCONV_W, D_C)), resident((1, D_C)),
        ] + [pl.BlockSpec(memory_space=pl.ANY)] * 3,
        out_specs=[prev(D_MODEL), per_seq((H_B, HEAD_DIM, N_STATE)), per_seq((8, CONV_DIM)),
                   per_seq((8, D_C))],
        out_shape=[jax.ShapeDtypeStruct((m, D_MODEL), F32)]
        + [jax.ShapeDtypeStruct(o.shape, F32) for o in outs],
        input_output_aliases={22: 1, 23: 2, 24: 3},
        scratch_shapes=[pltpu.VMEM((tm, D_MODEL), F32),
                        pltpu.VMEM((2, tm, D_B + D_C), BF16),
                        pltpu.VMEM((SSD_CHUNK + 8, CONV_DIM), F32),
                        pltpu.VMEM((SSD_CHUNK + 8, D_C), F32),
                        pltpu.VMEM((H_B, HEAD_DIM, N_STATE), F32)],
        compiler_params=pltpu.CompilerParams(
            dimension_semantics=("arbitrary",), vmem_limit_bytes=VMEM_LIMIT),
        name="ssd_post",
    )(x2d, oa0, oa1, w_out_a, w_out_bc, gpost, gpre, gmlp, w_up, w_dn,
      proj, proj, proj, proj, cw, cb, alog, dtb, dsk, sn, scw, scn, *outs)


def _pack_w_in_kernel(wt_ref, o_ref, *, cols):
    piece = 128
    moves = []
    for name, dst in (("xbc", COL_XBC), ("z", COL_Z), ("q", COL_Q), ("k", COL_Q + D_A),
                      ("v", COL_Q + 2 * D_A), ("b", COL_G), ("c", COL_G + D_C),
                      ("u", COL_G + 2 * D_C)):
        lo, hi = cols[name]
        moves += [(lo + j, dst + j) for j in range(0, hi - lo, piece)]
    for src, dst in moves:
        o_ref[:, dst:dst + piece] = wt_ref[src:src + piece, :].T.astype(BF16)
    lo = cols["dt"][0]
    blk = wt_ref[lo:lo + piece, :].T
    lane = lax.broadcasted_iota(jnp.int32, blk.shape, 1)
    o_ref[:, COL_DT:COL_DT + piece] = jnp.where(lane < H_B, blk, 0.0).astype(BF16)


def _pack_w_in(w_in, cols):
    depth, k, n = w_in.shape
    w_t = jnp.transpose(w_in, (0, 2, 1))
    return pl.pallas_call(
        functools.partial(_pack_w_in_kernel, cols=cols),
        grid=(depth,),
        in_specs=[pl.BlockSpec((None, n, k), lambda l: (l, 0, 0))],
        out_specs=pl.BlockSpec((None, k, PROJ_W), lambda l: (l, 0, 0)),
        out_shape=jax.ShapeDtypeStruct((depth, k, PROJ_W), BF16),
        compiler_params=pltpu.CompilerParams(
            dimension_semantics=("parallel",), vmem_limit_bytes=VMEM_LIMIT),
        name="pack_w_in",
    )(w_t)


def _prep_weights(norm_mix_pre, norm_mix_post, norm_mlp_pre, norm_mlp_post, w_in, w_out,
                  attn_norm, ssm_conv_w, ssm_conv_b, ssm_a_log, ssm_dt_bias, ssm_d, ssm_norm,
                  sconv_w, sconv_norm, w_mlp_up, w_mlp_down):
    depth = w_in.shape[0]
    o = 0
    cols = {}
    for name, width in (("q", D_A), ("k", D_A), ("v", D_A), ("z", D_B), ("xbc", CONV_DIM),
                        ("dt", H_B), ("b", D_C), ("c", D_C), ("u", D_C)):
        cols[name] = (o, o + width)
        o += width

    w_in_r = _pack_w_in(w_in, cols)

    def lane_pad(v):
        return jnp.pad(v.astype(F32), ((0, 0), (0, 128 - H_B)))[:, None, :]

    def row(v):
        return v.astype(F32)[:, None, :]

    return dict(
        g_mix_pre=row(norm_mix_pre), g_mix_post=row(norm_mix_post),
        g_mlp_pre=row(norm_mlp_pre), g_mlp_post=row(norm_mlp_post),
        w_in=w_in_r,
        w_out_a=w_out[:, 0:D_A, :].astype(BF16),
        w_out_bc=w_out[:, D_A:, :].astype(BF16),
        attn_norm=row(attn_norm),
        ssd=(ssm_conv_w.astype(F32), row(ssm_conv_b), lane_pad(ssm_a_log),
             lane_pad(ssm_dt_bias), row(jnp.repeat(ssm_d, HEAD_DIM, axis=1)),
             row(ssm_norm), sconv_w.astype(F32), row(sconv_norm)),
        w_up=w_mlp_up.astype(BF16), w_dn=w_mlp_down.astype(BF16),
    )


def _run_trunk(x, states, w, prompt):
    batch, seq, _ = x.shape
    depth = w["w_in"].shape[0]
    m = batch * seq
    x2d = x.reshape(m, D_MODEL)
    tm = 512 if m % 512 == 0 else m
    if prompt:
        wb = min(WIN_MAX, seq)
    else:
        cache_k, cache_v, st_ssm, st_conv, st_sconv = states
        wb = cache_k.shape[2]
        cache_k = jnp.transpose(cache_k, (0, 1, 3, 4, 2)).reshape(depth, batch, D_A, wb)
        cache_v = jnp.transpose(cache_v, (0, 1, 3, 4, 2)).reshape(depth, batch, D_A, wb)
        h0 = st_ssm.astype(F32)
        cprev8 = jnp.pad(st_conv.astype(F32), ((0, 0), (0, 0), (8 - (CONV_W - 1), 0), (0, 0)))
        sprev8 = jnp.pad(st_sconv.astype(F32), ((0, 0), (0, 0), (8 - (SCONV_W - 1), 0), (0, 0)))
    new_k = lax.empty((depth, batch, D_A, wb), F32)
    new_v = lax.empty((depth, batch, D_A, wb), F32)
    st_out = (lax.empty((depth, batch, H_B, HEAD_DIM, N_STATE), F32),
              lax.empty((depth, batch, 8, CONV_DIM), F32),
              lax.empty((depth, batch, 8, D_C), F32))
    for l in range(depth):
        if prompt:
            proj, qkv_perm, new_k, new_v = _in_proj(x2d, w["g_mix_pre"], w["w_in"], l,
                                                    IN_PROJ_TM if seq % IN_PROJ_TM == 0 else tm,
                                                    seq, wb, new_k, new_v)
            oa0, oa1 = _attn_prompt(qkv_perm, w["attn_norm"], l, batch, seq)
            x2d, *st_out = _fused(x2d, oa0, oa1, proj, w["w_out_a"], w["w_out_bc"],
                                  w["g_mix_post"], w["g_mlp_pre"], w["g_mlp_post"],
                                  w["w_up"], w["w_dn"], w["ssd"], st_out, l, batch, tm, 512)
        else:
            (proj,) = _in_proj(x2d, w["g_mix_pre"], w["w_in"], l, tm)
            oa0, oa1, new_k, new_v = _attn_sample(
                proj, cache_k, cache_v, w["attn_norm"], new_k, new_v, l, batch, seq, wb,
                ATTN_SAMPLE_GROUP if batch % ATTN_SAMPLE_GROUP == 0 else 1)
            ybc, *st_out = _ssd(proj, h0, cprev8, sprev8, w["ssd"], st_out, l, batch, 1, seq,
                                SSD_GROUP if batch % SSD_GROUP == 0 else 1)
            ybc = ybc.reshape(m, D_B + D_C)
            x2d = _post(x2d, oa0, oa1, ybc, w["w_out_a"], w["w_out_bc"], w["g_mix_post"],
                        w["g_mlp_pre"], w["g_mlp_post"], w["w_up"], w["w_dn"], l, tm, 1024)
    h_all, c_all, s_all = st_out
    outs = (jnp.transpose(new_k.reshape(depth, batch, H_A, HEAD_DIM, wb), (0, 1, 4, 2, 3)),
            jnp.transpose(new_v.reshape(depth, batch, H_A, HEAD_DIM, wb), (0, 1, 4, 2, 3)),
            h_all, c_all[:, :, 8 - (CONV_W - 1):, :], s_all[:, :, 8 - (SCONV_W - 1):, :])
    return x2d.reshape(batch, seq, D_MODEL), outs


def kernel(x_prompt, x_sample, cache_attn_k, cache_attn_v, state_ssm, state_ssm_conv, state_sconv, norm_mix_pre, norm_mix_post, norm_mlp_pre, norm_mlp_post, w_in, w_out, attn_norm, ssm_conv_w, ssm_conv_b, ssm_a_log, ssm_dt_bias, ssm_d, ssm_norm, sconv_w, sconv_norm, w_mlp_up, w_mlp_down):
    w = _prep_weights(norm_mix_pre, norm_mix_post, norm_mlp_pre, norm_mlp_post, w_in, w_out,
                      attn_norm, ssm_conv_w, ssm_conv_b, ssm_a_log, ssm_dt_bias, ssm_d,
                      ssm_norm, sconv_w, sconv_norm, w_mlp_up, w_mlp_down)
    y_prompt, (p_k, p_v, p_ssm, p_conv, p_sconv) = _run_trunk(x_prompt, None, w, True)
    states = (cache_attn_k, cache_attn_v, state_ssm, state_ssm_conv, state_sconv)
    y_sample, (s_k, s_v, s_ssm, s_conv, s_sconv) = _run_trunk(x_sample, states, w, False)
    return (y_prompt, y_sample, p_k, p_v, p_ssm, p_conv, p_sconv,
            s_k, s_v, s_ssm, s_conv, s_sconv)
```
